```python
import math
import jax, jax.numpy as jnp
from jax import lax
import numpy as np

D_MODEL = 1024
BATCH = 2
SEQ = 8192
DEPTH = 1

M_HEADS = 4
M_HEAD_DIM = D_MODEL // M_HEADS
M_WIDTH = M_HEADS * M_HEAD_DIM
M_CONV = 4
M_CHUNK = 128
R_HEAD_DIM = 64
R_WIDTH = D_MODEL
R_HEADS = R_WIDTH // R_HEAD_DIM
R_DECAY_RANK = 64
R_AAA_RANK = 64
R_GATE_RANK = 128
R_GN_EPS = 64e-5
D_FF = 128 * ((8 * D_MODEL // 3 + 127) // 128)
LN_EPS = 1e-5
ALPHA = (2 * DEPTH) ** 0.25
BETA = (8 * DEPTH) ** -0.25
R_COLS = 3 * R_WIDTH + R_DECAY_RANK + R_AAA_RANK + R_GATE_RANK
W_IN_SPLITS = (M_WIDTH, M_WIDTH, M_WIDTH, M_WIDTH, M_HEADS, M_HEADS, R_COLS, D_MODEL, D_MODEL)
W_IN_COLS = 4 * M_WIDTH + 2 * M_HEADS + R_COLS + 2 * D_MODEL
R_SPLITS = (R_WIDTH, R_WIDTH, R_WIDTH, R_DECAY_RANK, R_AAA_RANK, R_GATE_RANK)

kernel_name = 'hybrid_mlstm_rwkv7_macaron_deepnorm'


def _offsets(sizes):
    return [int(o) for o in np.cumsum(np.asarray(sizes))[:-1]]


def layer_norm(x, g, b, eps=LN_EPS):
    xf = x.astype(jnp.float32)
    mu = jnp.mean(xf, axis=-1, keepdims=True)
    var = jnp.mean(jnp.square(xf - mu), axis=-1, keepdims=True)
    return ((xf - mu) * lax.rsqrt(var + eps) * g + b).astype(x.dtype)


def swiglu(x, w_gate, w_up, w_down):
    return (jax.nn.silu(x @ w_gate) * (x @ w_up)) @ w_down


def causal_dwconv(x, w, b):
    k_w = w.shape[0]
    t = x.shape[1]
    xp = jnp.pad(x, ((0, 0), (k_w - 1, 0), (0, 0)))
    out = b
    for j in range(k_w):
        out = out + xp[:, j:j + t] * w[j]
    return out


def mlstm_chunkwise(q, k, v, i_pre, log_f):
    bsz, nh, t, dh = q.shape
    nc = t // M_CHUNK
    q = q.reshape(bsz, nh, nc, M_CHUNK, dh) * dh ** -0.5
    k = k.reshape(bsz, nh, nc, M_CHUNK, dh)
    v = v.reshape(bsz, nh, nc, M_CHUNK, dh)
    i_pre = i_pre.reshape(bsz, nh, nc, M_CHUNK)
    b = jnp.cumsum(log_f.reshape(bsz, nh, nc, M_CHUNK), axis=-1)
    g = b[..., -1]
    a = g[..., None] - b + i_pre
    m_loc = jnp.max(a, axis=-1)
    wa = jnp.exp(a - m_loc[..., None])
    c_loc = jnp.einsum('bhcs,bhcsk,bhcsv->bhckv', wa, k, v)
    n_loc = jnp.einsum('bhcs,bhcsk->bhck', wa, k)

    def step(carry, xs):
        c_st, n_st, m_st = carry
        g_c, m_c, cc, nn = xs
        m_new = jnp.maximum(g_c + m_st, m_c)
        s_old = jnp.exp(g_c + m_st - m_new)
        s_new = jnp.exp(m_c - m_new)
        c_next = s_old[..., None, None] * c_st + s_new[..., None, None] * cc
        n_next = s_old[..., None] * n_st + s_new[..., None] * nn
        return (c_next, n_next, m_new), (c_st, n_st, m_st)

    init = (jnp.zeros((bsz, nh, dh, dh), q.dtype), jnp.zeros((bsz, nh, dh), q.dtype),
            jnp.zeros((bsz, nh), q.dtype))
    xs = (jnp.moveaxis(g, 2, 0), jnp.moveaxis(m_loc, 2, 0), jnp.moveaxis(c_loc, 2, 0),
          jnp.moveaxis(n_loc, 2, 0))
    _, (c_prev, n_prev, m_prev) = lax.scan(step, init, xs)
    c_prev = jnp.moveaxis(c_prev, 0, 2)
    n_prev = jnp.moveaxis(n_prev, 0, 2)
    m_prev = jnp.moveaxis(m_prev, 0, 2)

    causal = jnp.tril(jnp.ones((M_CHUNK, M_CHUNK), dtype=bool))
    d_log = b[..., :, None] - b[..., None, :] + i_pre[..., None, :]
    d_log = jnp.where(causal, d_log, -jnp.inf)
    inter = b + m_prev[..., None]
    m_t = jnp.maximum(jnp.max(d_log, axis=-1), inter)
    s = jnp.einsum('bhctd,bhcsd->bhcts', q, k) * jnp.exp(d_log - m_t[..., None])
    s_inter = jnp.exp(inter - m_t)
    num = (jnp.einsum('bhcts,bhcsv->bhctv', s, v)
           + s_inter[..., None] * jnp.einsum('bhctk,bhckv->bhctv', q, c_prev))
    den = jnp.sum(s, axis=-1) + s_inter * jnp.einsum('bhctk,bhck->bhct', q, n_prev)
    h = num / jnp.maximum(jnp.abs(den), jnp.exp(-m_t))[..., None]
    return h.reshape(bsz, nh, t, dh)


def rwkv7_recurrence(r, w, k, v, a, b):
    bsz, _, nh, n = r.shape

    def step(st, xs):
        r_t, w_t, k_t, v_t, a_t, b_t = xs
        sa = jnp.einsum('bhvk,bhk->bhv', st, a_t)
        st = st * w_t[:, :, None, :] + sa[..., None] * b_t[:, :, None, :] + v_t[..., None] * k_t[:, :, None, :]
        return st, jnp.einsum('bhvk,bhk->bhv', st, r_t)

    xs = tuple(jnp.moveaxis(z, 1, 0) for z in (r, w, k, v, a, b))
    _, y = lax.scan(step, jnp.zeros((bsz, nh, n, n), r.dtype), xs)
    return jnp.moveaxis(y, 0, 1)


def token_mixer(h, w_in, m_conv_w, m_conv_b, m_i_bias, m_f_bias, m_norm_g, r_mu, r_w0, r_w2,
                r_a0, r_a2, r_g2, r_k_k, r_k_a, r_r_k, r_gn_g, r_gn_b, w_branch_a, w_branch_b, w_out):
    bsz, t, _ = h.shape
    f32 = jnp.float32
    p = h @ w_in
    mq, mk, mv, mo, mi, mf, rc, gate_a, gate_b = jnp.split(p, _offsets(W_IN_SPLITS), axis=-1)

    qk = jax.nn.silu(causal_dwconv(jnp.concatenate([mq, mk], axis=-1), m_conv_w, m_conv_b))
    mq, mk = jnp.split(qk, 2, axis=-1)

    def mheads(z):
        return z.reshape(bsz, t, M_HEADS, M_HEAD_DIM).transpose(0, 2, 1, 3).astype(f32)

    i_pre = (mi + m_i_bias).astype(f32).transpose(0, 2, 1)
    log_f = jax.nn.log_sigmoid((mf + m_f_bias).astype(f32)).transpose(0, 2, 1)
    hm = mlstm_chunkwise(mheads(mq), mheads(mk), mheads(mv), i_pre, log_f)
    mu = jnp.mean(hm, axis=-1, keepdims=True)
    var = jnp.mean(jnp.square(hm - mu), axis=-1, keepdims=True)
    hm = ((hm - mu) * lax.rsqrt(var + LN_EPS)).transpose(0, 2, 1, 3).reshape(bsz, t, M_WIDTH).astype(h.dtype)
    hm = hm * m_norm_g * jax.nn.sigmoid(mo)
    y_a = hm @ w_branch_a

    rc_prev = jnp.pad(rc, ((0, 0), (1, 0), (0, 0)))[:, :-1]
    rc = rc + (rc_prev - rc) * r_mu
    rr, rk, rv, wd, ad, gd = jnp.split(rc, _offsets(R_SPLITS), axis=-1)
    w_log = -jax.nn.softplus(-(r_w0 + jnp.tanh(wd) @ r_w2)) - 0.5
    decay = jnp.exp(-jnp.exp(w_log.astype(f32)))
    a = jax.nn.sigmoid(r_a0 + ad @ r_a2)
    g = jax.nn.sigmoid(gd) @ r_g2

    def rheads(z):
        return z.reshape(bsz, t, R_HEADS, R_HEAD_DIM).astype(f32)

    kk = rheads(rk * r_k_k)
    kk = kk / jnp.maximum(jnp.sqrt(jnp.sum(jnp.square(kk), axis=-1, keepdims=True)), 1e-12)
    rk = rk * (1 + (a - 1) * r_k_a)
    r_h, k_h, v_h, a_h = rheads(rr), rheads(rk), rheads(rv), rheads(a)
    yr = rwkv7_recurrence(r_h, rheads(decay), k_h, v_h, -kk, kk * a_h)
    mu = jnp.mean(yr, axis=-1, keepdims=True)
    var = jnp.mean(jnp.square(yr - mu), axis=-1, keepdims=True)
    yr = ((yr - mu) * lax.rsqrt(var + R_GN_EPS)).reshape(bsz, t, R_WIDTH) * r_gn_g + r_gn_b
    bonus = jnp.sum(r_h * k_h * r_r_k, axis=-1, keepdims=True) * v_h
    yr = (yr + bonus.reshape(bsz, t, R_WIDTH)).astype(h.dtype) * g
    y_b = yr @ w_branch_b

    merged = jax.nn.sigmoid(gate_a) * y_a + jax.nn.sigmoid(gate_b) * y_b
    return merged @ w_out


def setup_inputs(seed: int = 0) -> dict:
    key = jax.random.key(seed)
    ks = iter(jax.random.split(key, 48))
    d = D_MODEL

    def nrm(shape, std):
        return jax.random.normal(next(ks), (DEPTH,) + shape, jnp.float32) * std

    x = jax.random.normal(next(ks), (BATCH, SEQ, d), jnp.float32)
    ffn1_w_gate = nrm((d, D_FF), d ** -0.5)
    ffn1_w_up = nrm((d, D_FF), BETA * d ** -0.5)
    ffn1_w_down = nrm((D_FF, d), BETA * D_FF ** -0.5)
    ln1_g = 1.0 + nrm((d,), 0.02)
    ln1_b = nrm((d,), 0.02)
    col_scale = np.ones((W_IN_COLS,), np.float32)
    mv0 = 2 * M_WIDTH
    col_scale[mv0:mv0 + M_WIDTH] = BETA
    rv0 = 4 * M_WIDTH + 2 * M_HEADS + 2 * R_WIDTH
    col_scale[rv0:rv0 + R_WIDTH] = BETA
    w_in = nrm((d, W_IN_COLS), d ** -0.5) * jnp.asarray(col_scale)
    m_conv_w = nrm((M_CONV, 2 * M_WIDTH), M_CONV ** -0.5)
    m_conv_b = nrm((2 * M_WIDTH,), 0.02)
    m_i_bias = nrm((M_HEADS,), 0.1)
    m_f_bias = jnp.linspace(3.0, 6.0, M_HEADS, dtype=jnp.float32) + nrm((M_HEADS,), 0.1)
    m_norm_g = 1.0 + nrm((M_WIDTH,), 0.02)
    r_mu = jax.random.uniform(next(ks), (DEPTH, R_COLS), jnp.float32, 0.2, 0.8)
    ramp = jnp.arange(R_WIDTH, dtype=jnp.float32) / (R_WIDTH - 1)
    r_w0 = -6.0 + 5.0 * ramp ** 0.85 + nrm((R_WIDTH,), 0.1)
    r_w2 = nrm((R_DECAY_RANK, R_WIDTH), 0.1 * R_DECAY_RANK ** -0.5)
    r_a0 = nrm((R_WIDTH,), 0.1)
    r_a2 = nrm((R_AAA_RANK, R_WIDTH), 0.1 * R_AAA_RANK ** -0.5)
    r_g2 = nrm((R_GATE_RANK, R_WIDTH), R_GATE_RANK ** -0.5)
    r_k_k = 0.85 + nrm((R_WIDTH,), 0.02)
    r_k_a = 1.0 + nrm((R_WIDTH,), 0.02)
    r_r_k = nrm((R_HEADS, R_HEAD_DIM), 0.1)
    r_gn_g = 1.0 + nrm((R_WIDTH,), 0.02)
    r_gn_b = nrm((R_WIDTH,), 0.02)
    w_branch_a = nrm((M_WIDTH, d), BETA * M_WIDTH ** -0.5)
    w_branch_b = nrm((R_WIDTH, d), BETA * R_WIDTH ** -0.5)
    w_out = nrm((d, d), BETA * d ** -0.5)
    ln2_g = 1.0 + nrm((d,), 0.02)
    ln2_b = nrm((d,), 0.02)
    ffn2_w_gate = nrm((d, D_FF), d ** -0.5)
    ffn2_w_up = nrm((d, D_FF), BETA * d ** -0.5)
    ffn2_w_down = nrm((D_FF, d), BETA * D_FF ** -0.5)
    ln3_g = 1.0 + nrm((d,), 0.02)
    ln3_b = nrm((d,), 0.02)
    return {'x': x, 'ffn1_w_gate': ffn1_w_gate, 'ffn1_w_up': ffn1_w_up, 'ffn1_w_down': ffn1_w_down,
            'ln1_g': ln1_g, 'ln1_b': ln1_b, 'w_in': w_in, 'm_conv_w': m_conv_w, 'm_conv_b': m_conv_b,
            'm_i_bias': m_i_bias, 'm_f_bias': m_f_bias, 'm_norm_g': m_norm_g, 'r_mu': r_mu, 'r_w0': r_w0,
            'r_w2': r_w2, 'r_a0': r_a0, 'r_a2': r_a2, 'r_g2': r_g2, 'r_k_k': r_k_k, 'r_k_a': r_k_a,
            'r_r_k': r_r_k, 'r_gn_g': r_gn_g, 'r_gn_b': r_gn_b, 'w_branch_a': w_branch_a,
            'w_branch_b': w_branch_b, 'w_out': w_out, 'ln2_g': ln2_g, 'ln2_b': ln2_b,
            'ffn2_w_gate': ffn2_w_gate, 'ffn2_w_up': ffn2_w_up, 'ffn2_w_down': ffn2_w_down,
            'ln3_g': ln3_g, 'ln3_b': ln3_b}


def reference(x, ffn1_w_gate, ffn1_w_up, ffn1_w_down, ln1_g, ln1_b, w_in, m_conv_w, m_conv_b,
              m_i_bias, m_f_bias, m_norm_g, r_mu, r_w0, r_w2, r_a0, r_a2, r_g2, r_k_k, r_k_a, r_r_k,
              r_gn_g, r_gn_b, w_branch_a, w_branch_b, w_out, ln2_g, ln2_b, ffn2_w_gate, ffn2_w_up,
              ffn2_w_down, ln3_g, ln3_b):
    for l in range(DEPTH):
        x = layer_norm(ALPHA * x + 0.5 * swiglu(x, ffn1_w_gate[l], ffn1_w_up[l], ffn1_w_down[l]),
                       ln1_g[l], ln1_b[l])
        mix = token_mixer(x, w_in[l], m_conv_w[l], m_conv_b[l], m_i_bias[l], m_f_bias[l], m_norm_g[l],
                          r_mu[l], r_w0[l], r_w2[l], r_a0[l], r_a2[l], r_g2[l], r_k_k[l], r_k_a[l],
                          r_r_k[l], r_gn_g[l], r_gn_b[l], w_branch_a[l], w_branch_b[l], w_out[l])
        x = layer_norm(ALPHA * x + mix, ln2_g[l], ln2_b[l])
        x = layer_norm(ALPHA * x + 0.5 * swiglu(x, ffn2_w_gate[l], ffn2_w_up[l], ffn2_w_down[l]),
                       ln3_g[l], ln3_b[l])
    return x
```

```python
import functools

import jax
import jax.numpy as jnp
from jax import lax
from jax.experimental import pallas as pl
from jax.experimental.pallas import tpu as pltpu

F32 = jnp.float32
BF16 = jnp.bfloat16

M_HEADS = 4
M_CONV = 4
M_CHUNK = 128
R_HEAD_DIM = 64
R_DECAY_RANK = 64
R_AAA_RANK = 64
R_GATE_RANK = 128
R_GN_EPS = 64e-5
LN_EPS = 1e-5

R_CHUNK = 64
R_GROUP_HEADS = 4
R_CHUNKS_PER_STEP = 1
VMEM_LIMIT = 56 * 1024 * 1024


def _bdot(a, b):
    return jnp.dot(a.astype(BF16), b.astype(BF16), preferred_element_type=F32)


def _bdot_nt(a, b):
    return lax.dot_general(a.astype(BF16), b.astype(BF16), (((1,), (1,)), ((), ())),
                           preferred_element_type=F32)


def _bdot_tn(a, b):
    return lax.dot_general(a.astype(BF16), b.astype(BF16), (((0,), (0,)), ((), ())),
                           preferred_element_type=F32)


def _split3(x):
    hi = x.astype(BF16)
    r1 = x - hi.astype(F32)
    mid = r1.astype(BF16)
    lo = (r1 - mid.astype(F32)).astype(BF16)
    return hi, mid, lo


def _dot_exact_lhs(m01, x):
    m = m01.astype(BF16)
    hi, mid, lo = _split3(x)
    return (jnp.dot(m, hi, preferred_element_type=F32) + jnp.dot(m, mid, preferred_element_type=F32)
            + jnp.dot(m, lo, preferred_element_type=F32))


def _dot_exact_rhs(x, m01):
    m = m01.astype(BF16)
    hi, mid, lo = _split3(x)
    return (jnp.dot(hi, m, preferred_element_type=F32) + jnp.dot(mid, m, preferred_element_type=F32)
            + jnp.dot(lo, m, preferred_element_type=F32))


def _dot_hp(a, b):
    a_hi = a.astype(BF16)
    a_lo = (a - a_hi.astype(F32)).astype(BF16)
    b_hi = b.astype(BF16)
    b_lo = (b - b_hi.astype(F32)).astype(BF16)
    return (jnp.dot(a_hi, b_hi, preferred_element_type=F32) + jnp.dot(a_lo, b_hi, preferred_element_type=F32)
            + jnp.dot(a_hi, b_lo, preferred_element_type=F32))


def _sigmoid(x):
    return 1.0 / (1.0 + jnp.exp(-x))


def _log_sigmoid(x):
    return jnp.minimum(x, 0.0) - jnp.log1p(jnp.exp(-jnp.abs(x)))


def _layer_norm(y, g, b, eps):
    mu = jnp.mean(y, axis=-1, keepdims=True)
    d = y - mu
    var = jnp.mean(d * d, axis=-1, keepdims=True)
    return d * lax.rsqrt(var + eps) * g + b


def _ffn_ln_kernel(x_ref, wg_ref, wu_ref, wd_ref, g_ref, b_ref, o_ref, ob_ref, h_ref, *, alpha, tf):
    x = x_ref[...]
    xb = x.astype(BF16)
    d_ff = wg_ref.shape[1]
    for c in range(d_ff // tf):
        sl = slice(c * tf, (c + 1) * tf)
        gate = jnp.dot(xb, wg_ref[:, sl], preferred_element_type=F32)
        up = jnp.dot(xb, wu_ref[:, sl], preferred_element_type=F32)
        h_ref[:, sl] = (gate * _sigmoid(gate) * up).astype(BF16)
    y = alpha * x + 0.5 * jnp.dot(h_ref[...], wd_ref[...], preferred_element_type=F32)
    out = _layer_norm(y, g_ref[...], b_ref[...], LN_EPS)
    o_ref[...] = out
    ob_ref[...] = out.astype(BF16)


def _ffn_ln(x, wg, wu, wd, g, b, alpha, tm=512, tf=256):
    n, d = x.shape
    d_ff = wg.shape[1]
    const = lambda i: (0, 0)
    return pl.pallas_call(
        functools.partial(_ffn_ln_kernel, alpha=alpha, tf=tf),
        grid=(n // tm,),
        in_specs=[
            pl.BlockSpec((tm, d), lambda i: (i, 0)),
            pl.BlockSpec((d, d_ff), const, pipeline_mode=pl.Buffered(1)),
            pl.BlockSpec((d, d_ff), const, pipeline_mode=pl.Buffered(1)),
            pl.BlockSpec((d_ff, d), const, pipeline_mode=pl.Buffered(1)),
            pl.BlockSpec((1, d), const),
            pl.BlockSpec((1, d), const),
        ],
        out_specs=[pl.BlockSpec((tm, d), lambda i: (i, 0)), pl.BlockSpec((tm, d), lambda i: (i, 0))],
        out_shape=[jax.ShapeDtypeStruct((n, d), F32), jax.ShapeDtypeStruct((n, d), BF16)],
        scratch_shapes=[pltpu.VMEM((tm, d_ff), BF16)],
        compiler_params=pltpu.CompilerParams(dimension_semantics=("arbitrary",), vmem_limit_bytes=VMEM_LIMIT),
        name="ffn_ln",
    )(x, wg, wu, wd, g, b)


def _proj_kernel(x_ref, w_ref, o_ref):
    o_ref[...] = jnp.dot(x_ref[...], w_ref[...], preferred_element_type=F32).astype(o_ref.dtype)


def _proj(xb, w, out_dtype, tm=512):
    n, d = xb.shape
    nc = w.shape[1]
    return pl.pallas_call(
        _proj_kernel,
        grid=(n // tm,),
        in_specs=[pl.BlockSpec((tm, d), lambda i: (i, 0)),
                  pl.BlockSpec((d, nc), lambda i: (0, 0), pipeline_mode=pl.Buffered(1))],
        out_specs=pl.BlockSpec((tm, nc), lambda i: (i, 0)),
        out_shape=jax.ShapeDtypeStruct((n, nc), out_dtype),
        compiler_params=pltpu.CompilerParams(dimension_semantics=("arbitrary",), vmem_limit_bytes=VMEM_LIMIT),
        name="proj",
    )(xb, w)


def _mlstm_kernel(q_ref, k_ref, v_ref, o_ref, g_ref, cw_ref, cb_ref, gb_ref, ng_ref, out_ref,
                  c_ref, n_ref, m_ref, pq_ref, pk_ref, *, nh, dh):
    L = q_ref.shape[0]
    width = nh * dh
    step = pl.program_id(1)

    @pl.when(step == 0)
    def _():
        c_ref[...] = jnp.zeros_like(c_ref)
        n_ref[...] = jnp.zeros_like(n_ref)
        m_ref[...] = jnp.zeros_like(m_ref)
        pq_ref[...] = jnp.zeros_like(pq_ref)
        pk_ref[...] = jnp.zeros_like(pk_ref)

    row = lax.broadcasted_iota(jnp.int32, (L, 1), 0)

    def conv_silu(cur, prev, w_off):
        acc = cb_ref[:, w_off:w_off + width] + cur * cw_ref[M_CONV - 1:M_CONV, w_off:w_off + width]
        for j in range(1, M_CONV):
            shifted = jnp.where(row >= j, pltpu.roll(cur, j, 0), pltpu.roll(prev, j, 0))
            acc = acc + shifted * cw_ref[M_CONV - 1 - j:M_CONV - j, w_off:w_off + width]
        return acc * _sigmoid(acc)

    q_raw = q_ref[...].astype(F32)
    k_raw = k_ref[...].astype(F32)
    q_all = conv_silu(q_raw, pq_ref[...], 0) * (dh ** -0.5)
    k_all = conv_silu(k_raw, pk_ref[...], width)
    pq_ref[...] = q_raw
    pk_ref[...] = k_raw

    gates = g_ref[...] + gb_ref[...]
    log_f = _log_sigmoid(gates)
    ri = lax.broadcasted_iota(jnp.int32, (L, L), 0)
    ci = lax.broadcasted_iota(jnp.int32, (L, L), 1)
    causal = ri >= ci
    b_cols = _dot_exact_lhs(causal.astype(F32), log_f)
    gates_t = gates.T
    b_rows = b_cols.T

    for h in range(nh):
        hs = slice(h * dh, (h + 1) * dh)
        q = q_all[:, hs]
        k = k_all[:, hs]
        v = v_ref[:, hs]
        i_col = gates[:, h:h + 1]
        b_col = b_cols[:, nh + h:nh + h + 1]
        i_row = gates_t[h:h + 1, :]
        b_row = b_rows[nh + h:nh + h + 1, :]
        g_tot = b_col[L - 1:L, :]
        c_prev = c_ref[h]
        n_prev = n_ref[8 * h:8 * h + 1, :]
        m_prev = m_ref[8 * h:8 * h + 1, 0:1]

        a_col = g_tot - b_col + i_col
        m_loc = jnp.max(a_col, axis=0, keepdims=True)
        kw = k * jnp.exp(a_col - m_loc)
        c_loc = _bdot_tn(kw, v)
        n_loc = jnp.sum(kw, axis=0, keepdims=True)

        d_log = jnp.where(causal, b_col - b_row + i_row, -jnp.inf)
        inter = b_col + m_prev
        m_t = jnp.maximum(jnp.max(d_log, axis=-1, keepdims=True), inter)
        s = _bdot_nt(q, k) * jnp.exp(d_log - m_t)
        s_inter = jnp.exp(inter - m_t)
        num = _bdot(s, v) + s_inter * _bdot(q, c_prev)
        den = jnp.sum(s, axis=-1, keepdims=True) + s_inter * jnp.sum(q * n_prev, axis=-1, keepdims=True)
        hid = num / jnp.maximum(jnp.abs(den), jnp.exp(-m_t))

        m_new = jnp.maximum(g_tot + m_prev, m_loc)
        s_old = jnp.exp(g_tot + m_prev - m_new)
        s_new = jnp.exp(m_loc - m_new)
        c_ref[h] = s_old * c_prev + s_new * c_loc
        n_ref[8 * h:8 * h + 1, :] = s_old * n_prev + s_new * n_loc
        m_ref[8 * h:8 * h + 1, :] = jnp.broadcast_to(m_new, (1, m_ref.shape[1]))

        mu = jnp.mean(hid, axis=-1, keepdims=True)
        d = hid - mu
        var = jnp.mean(d * d, axis=-1, keepdims=True)
        hn = d * lax.rsqrt(var + LN_EPS)
        out_ref[:, hs] = (hn * ng_ref[:, hs] * _sigmoid(o_ref[:, hs].astype(F32))).astype(out_ref.dtype)


def _mlstm(pm, pg, conv_w, conv_b, gate_bias, norm_g, bsz, t):
    n = pm.shape[0]
    width = pm.shape[1] // 4
    nh = M_HEADS
    dh = width // nh
    L = M_CHUNK
    nc = t // L
    rows = lambda b, c: b * nc + c
    const = lambda b, c: (0, 0)
    return pl.pallas_call(
        functools.partial(_mlstm_kernel, nh=nh, dh=dh),
        grid=(bsz, nc),
        in_specs=[
            pl.BlockSpec((L, width), lambda b, c: (rows(b, c), 0)),
            pl.BlockSpec((L, width), lambda b, c: (rows(b, c), 1)),
            pl.BlockSpec((L, width), lambda b, c: (rows(b, c), 2)),
            pl.BlockSpec((L, width), lambda b, c: (rows(b, c), 3)),
            pl.BlockSpec((L, 128), lambda b, c: (rows(b, c), 0)),
            pl.BlockSpec((M_CONV, 2 * width), const),
            pl.BlockSpec((1, 2 * width), const),
            pl.BlockSpec((1, 128), const),
            pl.BlockSpec((1, width), const),
        ],
        out_specs=pl.BlockSpec((L, width), lambda b, c: (rows(b, c), 0)),
        out_shape=jax.ShapeDtypeStruct((n, width), BF16),
        scratch_shapes=[
            pltpu.VMEM((nh, dh, dh), F32),
            pltpu.VMEM((8 * nh, dh), F32),
            pltpu.VMEM((8 * nh, 128), F32),
            pltpu.VMEM((L, width), F32),
            pltpu.VMEM((L, width), F32),
        ],
        compiler_params=pltpu.CompilerParams(dimension_semantics=("arbitrary", "arbitrary"),
                                             vmem_limit_bytes=VMEM_LIMIT),
        name="mlstm",
    )(pm, pm, pm, pm, pg, conv_w, conv_b, gate_bias, norm_g)


def _tri_solve(nm, smm):
    size = nm.shape[0]
    ri = lax.broadcasted_iota(jnp.int32, (size, size), 0)
    ci = lax.broadcasted_iota(jnp.int32, (size, size), 1)
    blk = lambda w: (ri // w) == (ci // w)
    eye = (ri == ci).astype(F32)
    nd = jnp.where(blk(16), nm, 0.0)
    t = eye + nd
    p2 = nd
    for _ in range(3):
        p2 = smm(p2, p2)
        t = t + smm(p2, t)
    for lo, hi in ((16, 32), (32, 64)):
        n_off = jnp.where(jnp.logical_and(blk(hi), jnp.logical_not(blk(lo))), nm, 0.0)
        t = t + smm(smm(t, n_off), t)
    return t


def _rwkv_kernel(xr_ref, xk_ref, xv_ref, xl_ref, mur_ref, muk_ref, muv_ref, mul_ref,
                 w0_ref, a0_ref, kkw_ref, kaw_ref, rrk_ref, gng_ref, gnb_ref,
                 w2_ref, a2_ref, g2_ref, out_ref,
                 s_ref, pr_ref, pk_ref, pv_ref, pl_ref, *, L, nchunk, hd):
    R = xr_ref.shape[0]
    gw = xr_ref.shape[1]
    hpg = gw // hd
    sl_rows = hpg * L
    step = pl.program_id(2)

    @pl.when(step == 0)
    def _():
        s_ref[...] = jnp.zeros_like(s_ref)
        pr_ref[...] = jnp.zeros_like(pr_ref)
        pk_ref[...] = jnp.zeros_like(pk_ref)
        pv_ref[...] = jnp.zeros_like(pv_ref)
        pl_ref[...] = jnp.zeros_like(pl_ref)

    row = lax.broadcasted_iota(jnp.int32, (R, 1), 0)

    def shift_mix(x_ref, p_ref, mu_ref):
        x = x_ref[...]
        prev = jnp.where(row == 0, p_ref[0:1, :], pltpu.roll(x, 1, 0))
        p_ref[0:1, :] = x[R - 1:R, :]
        return x + (prev - x) * mu_ref[...]

    rr = shift_mix(xr_ref, pr_ref, mur_ref)
    rk = shift_mix(xk_ref, pk_ref, muk_ref)
    rv = shift_mix(xv_ref, pv_ref, muv_ref)
    xl = shift_mix(xl_ref, pl_ref, mul_ref)

    wl = w0_ref[...] + _dot_hp(jnp.tanh(xl), w2_ref[...])
    lw = -jnp.exp(_log_sigmoid(wl) - 0.5)
    a = _sigmoid(a0_ref[...] + _bdot(xl, a2_ref[...]))
    g = _bdot(_sigmoid(xl), g2_ref[...])

    li = lax.broadcasted_iota(jnp.int32, (gw, gw), 0)
    lj = lax.broadcasted_iota(jnp.int32, (gw, gw), 1)
    head_ones = ((li // hd) == (lj // hd)).astype(F32)

    def seg_sum(z):
        return _dot_exact_rhs(z, head_ones)

    kk = rk * kkw_ref[...]
    kk = kk / jnp.maximum(jnp.sqrt(seg_sum(kk * kk)), 1e-12)
    k2 = rk * (1.0 + (a - 1.0) * kaw_ref[...])
    av = -kk
    bv = kk * a
    bonus = seg_sum(rr * k2 * rrk_ref[...]) * rv

    ti = lax.broadcasted_iota(jnp.int32, (R, R), 0)
    tj = lax.broadcasted_iota(jnp.int32, (R, R), 1)
    chunk_tri = jnp.logical_and(ti // L == tj // L, ti >= tj).astype(F32)
    cum = _dot_exact_lhs(chunk_tri, lw)

    si = lax.broadcasted_iota(jnp.int32, (sl_rows, gw), 0)
    sj = lax.broadcasted_iota(jnp.int32, (sl_rows, gw), 1)
    stack_mask = (si // L) == (sj // hd)
    qi = lax.broadcasted_iota(jnp.int32, (sl_rows, sl_rows), 0)
    qj = lax.broadcasted_iota(jnp.int32, (sl_rows, sl_rows), 1)
    strict = qi > qj
    incl = qi >= qj

    def stack(z):
        return jnp.where(stack_mask, jnp.concatenate([z] * hpg, axis=0), 0.0)

    def unstack(z):
        out = z[0:L]
        for h in range(1, hpg):
            out = out + z[h * L:(h + 1) * L]
        return out

    for c in range(nchunk):
        rs = slice(c * L, (c + 1) * L)
        cum_c = cum[rs]
        lw_c = lw[rs]
        cum_l = cum_c[L - 1:L, :]
        e_pos = jnp.exp(cum_c)
        e_neg = jnp.exp(-cum_c)
        e_end = jnp.exp(cum_l - cum_c)
        w_l = jnp.exp(cum_l)
        v_c = rv[rs]
        rt = stack(rr[rs] * e_pos)
        at = stack(av[rs] * jnp.exp(cum_c - lw_c))
        bt = stack(bv[rs] * e_neg)
        kt = stack(k2[rs] * e_neg)
        bh = stack(bv[rs] * e_end)
        kh = stack(k2[rs] * e_end)
        vs = stack(v_c)

        nm = jnp.where(strict, _bdot_nt(at, bt), 0.0)
        aak = jnp.where(strict, _bdot_nt(at, kt), 0.0)
        arb = jnp.where(incl, _bdot_nt(rt, bt), 0.0)
        ark = jnp.where(incl, _bdot_nt(rt, kt), 0.0)
        t = _tri_solve(nm, _bdot)
        a_p = _bdot(t, at)
        u_loc = _bdot(t, _bdot(aak, vs))
        r_p = unstack(rt + _bdot(arb, a_p))
        y_loc = unstack(_bdot(arb, u_loc) + _bdot(ark, vs))
        s0 = s_ref[...]
        y = _bdot_nt(r_p, s0) + y_loc
        s_ref[...] = (s0 * w_l + _bdot(s0, _bdot_tn(a_p, bh))
                      + _bdot_tn(u_loc, bh) + _bdot_tn(vs, kh))

        mu = seg_sum(y) * (1.0 / hd)
        d = y - mu
        var = seg_sum(d * d) * (1.0 / hd)
        yn = d * lax.rsqrt(var + R_GN_EPS) * gng_ref[...] + gnb_ref[...]
        out_ref[rs, :] = ((yn + bonus[rs]) * g[rs]).astype(out_ref.dtype)


def _rwkv(pr, r_mu, w0, a0, kkw, kaw, rrk, gng, gnb, w2p, a2p, g2p, bsz, t):
    n = pr.shape[0]
    width = w0.shape[1]
    hd = R_HEAD_DIM
    gw = R_GROUP_HEADS * hd
    ng = width // gw
    L = R_CHUNK
    R = L * R_CHUNKS_PER_STEP
    ns = t // R
    lr = w2p.shape[0]
    lblk = (3 * width) // lr
    rows = lambda b, c: b * ns + c
    xspec = lambda off: pl.BlockSpec((R, gw), lambda b, g, c: (rows(b, c), off * ng + g))
    pspec = lambda off: pl.BlockSpec((1, gw), lambda b, g, c: (0, off * ng + g))
    gspec = pl.BlockSpec((1, gw), lambda b, g, c: (0, g))
    wspec = pl.BlockSpec((lr, gw), lambda b, g, c: (0, g))
    return pl.pallas_call(
        functools.partial(_rwkv_kernel, L=L, nchunk=R_CHUNKS_PER_STEP, hd=hd),
        grid=(bsz, ng, ns),
        in_specs=[
            xspec(0), xspec(1), xspec(2),
            pl.BlockSpec((R, lr), lambda b, g, c: (rows(b, c), lblk)),
            pspec(0), pspec(1), pspec(2),
            pl.BlockSpec((1, lr), lambda b, g, c: (0, lblk)),
            gspec, gspec, gspec, gspec, gspec, gspec, gspec,
            wspec, wspec, wspec,
        ],
        out_specs=pl.BlockSpec((R, gw), lambda b, g, c: (rows(b, c), g)),
        out_shape=jax.ShapeDtypeStruct((n, width), BF16),
        scratch_shapes=[
            pltpu.VMEM((gw, gw), F32),
            pltpu.VMEM((8, gw), F32),
            pltpu.VMEM((8, gw), F32),
            pltpu.VMEM((8, gw), F32),
            pltpu.VMEM((8, lr), F32),
        ],
        compiler_params=pltpu.CompilerParams(dimension_semantics=("arbitrary", "arbitrary", "arbitrary"),
                                             vmem_limit_bytes=VMEM_LIMIT),
        name="rwkv",
    )(pr, pr, pr, pr, r_mu, r_mu, r_mu, r_mu, w0, a0, kkw, kaw, rrk, gng, gnb, w2p, a2p, g2p)


def _post_kernel(hm_ref, yr_ref, ga_ref, gb_ref, x_ref, wa_ref, wb_ref, wo_ref, g_ref, b_ref, o_ref, *, alpha):
    ya = jnp.dot(hm_ref[...], wa_ref[...], preferred_element_type=F32)
    yb = jnp.dot(yr_ref[...], wb_ref[...], preferred_element_type=F32)
    merged = _sigmoid(ga_ref[...].astype(F32)) * ya + _sigmoid(gb_ref[...].astype(F32)) * yb
    mix = jnp.dot(merged.astype(BF16), wo_ref[...], preferred_element_type=F32)
    o_ref[...] = _layer_norm(alpha * x_ref[...] + mix, g_ref[...], b_ref[...], LN_EPS)


def _post(hm, yr, pgate, x1, wa, wb, wo, g, b, alpha, tm=512):
    n, d = x1.shape
    tile = lambda j: pl.BlockSpec((tm, d), lambda i: (i, j))
    wspec = pl.BlockSpec((d, d), lambda i: (0, 0), pipeline_mode=pl.Buffered(1))
    vspec = pl.BlockSpec((1, d), lambda i: (0, 0))
    return pl.pallas_call(
        functools.partial(_post_kernel, alpha=alpha),
        grid=(n // tm,),
        in_specs=[tile(0), tile(0), tile(0), tile(1), tile(0), wspec, wspec, wspec, vspec, vspec],
        out_specs=tile(0),
        out_shape=jax.ShapeDtypeStruct((n, d), F32),
        compiler_params=pltpu.CompilerParams(dimension_semantics=("arbitrary",), vmem_limit_bytes=VMEM_LIMIT),
        name="post",
    )(hm, yr, pgate, pgate, x1, wa, wb, wo, g, b)


def kernel(x, ffn1_w_gate, ffn1_w_up, ffn1_w_down, ln1_g, ln1_b, w_in, m_conv_w, m_conv_b, m_i_bias, m_f_bias,
           m_norm_g, r_mu, r_w0, r_w2, r_a0, r_a2, r_g2, r_k_k, r_k_a, r_r_k, r_gn_g, r_gn_b, w_branch_a,
           w_branch_b, w_out, ln2_g, ln2_b, ffn2_w_gate, ffn2_w_up, ffn2_w_down, ln3_g, ln3_b):
    bsz, t, d = x.shape
    depth = w_in.shape[0]
    alpha = (2 * depth) ** 0.25
    mw = d
    rw = d
    nh = M_HEADS
    lr = R_DECAY_RANK + R_AAA_RANK + R_GATE_RANK
    row = lambda p: p.reshape(1, -1)

    cur = x.reshape(bsz * t, d)
    for l in range(depth):
        x1, x1b = _ffn_ln(cur, ffn1_w_gate[l].astype(BF16), ffn1_w_up[l].astype(BF16),
                          ffn1_w_down[l].astype(BF16), row(ln1_g[l]), row(ln1_b[l]), alpha)

        w = w_in[l]
        o_gate = 4 * mw
        o_rc = o_gate + 2 * nh
        o_ga = o_rc + 3 * rw + lr
        w_m = w[:, :o_gate].astype(BF16)
        w_g = jnp.pad(w[:, o_gate:o_rc], ((0, 0), (0, 128 - 2 * nh))).astype(BF16)
        w_r = w[:, o_rc:o_ga].astype(BF16)
        w_gate = w[:, o_ga:].astype(BF16)
        pm = _proj(x1b, w_m, BF16)
        pg = _proj(x1b, w_g, F32)
        pr = _proj(x1b, w_r, F32)
        pgate = _proj(x1b, w_gate, BF16)

        gate_bias = jnp.pad(jnp.concatenate([m_i_bias[l], m_f_bias[l]]), (0, 128 - 2 * nh)).reshape(1, 128)
        hm = _mlstm(pm, pg, m_conv_w[l], row(m_conv_b[l]), gate_bias, row(m_norm_g[l]), bsz, t)

        zeros = lambda r: jnp.zeros((r, rw), F32)
        w2p = jnp.concatenate([r_w2[l], zeros(R_AAA_RANK + R_GATE_RANK)], axis=0)
        a2p = jnp.concatenate([zeros(R_DECAY_RANK), r_a2[l], zeros(R_GATE_RANK)], axis=0)
        g2p = jnp.concatenate([zeros(R_DECAY_RANK + R_AAA_RANK), r_g2[l]], axis=0)
        yr = _rwkv(pr, row(r_mu[l]), row(r_w0[l]), row(r_a0[l]), row(r_k_k[l]), row(r_k_a[l]), row(r_r_k[l]),
                   row(r_gn_g[l]), row(r_gn_b[l]), w2p, a2p, g2p, bsz, t)

        x2 = _post(hm, yr, pgate, x1, w_branch_a[l].astype(BF16), w_branch_b[l].astype(BF16),
                   w_out[l].astype(BF16), row(ln2_g[l]), row(ln2_b[l]), alpha)
        cur, _ = _ffn_ln(x2, ffn2_w_gate[l].astype(BF16), ffn2_w_up[l].astype(BF16),
                         ffn2_w_down[l].astype(BF16), row(ln3_g[l]), row(ln3_b[l]), alpha)
    return cur.reshape(bsz, t, d)
```

```python
import functools

import jax
import jax.numpy as jnp
from jax import lax
from jax.experimental import pallas as pl
from jax.experimental.pallas import tpu as pltpu

F32 = jnp.float32
BF16 = jnp.bfloat16

M_HEADS = 4
M_CONV = 4
M_CHUNK = 128
R_HEAD_DIM = 64
R_DECAY_RANK = 64
R_AAA_RANK = 64
R_GATE_RANK = 128
R_GN_EPS = 64e-5
LN_EPS = 1e-5

R_CHUNK = 64
R_GROUP_HEADS = 4
R_CHUNKS_PER_STEP = 1
VMEM_LIMIT = 56 * 1024 * 1024


def _bdot(a, b):
    return jnp.dot(a.astype(BF16), b.astype(BF16), preferred_element_type=F32)


def _bdot_nt(a, b):
    return lax.dot_general(a.astype(BF16), b.astype(BF16), (((1,), (1,)), ((), ())),
                           preferred_element_type=F32)


def _bdot_tn(a, b):
    return lax.dot_general(a.astype(BF16), b.astype(BF16), (((0,), (0,)), ((), ())),
                           preferred_element_type=F32)


def _split3(x):
    hi = x.astype(BF16)
    r1 = x - hi.astype(F32)
    mid = r1.astype(BF16)
    lo = (r1 - mid.astype(F32)).astype(BF16)
    return hi, mid, lo


def _dot_exact_lhs(m01, x):
    m = m01.astype(BF16)
    hi, mid, lo = _split3(x)
    return (jnp.dot(m, hi, preferred_element_type=F32) + jnp.dot(m, mid, preferred_element_type=F32)
            + jnp.dot(m, lo, preferred_element_type=F32))


def _dot_exact_rhs(x, m01):
    m = m01.astype(BF16)
    hi, mid, lo = _split3(x)
    return (jnp.dot(hi, m, preferred_element_type=F32) + jnp.dot(mid, m, preferred_element_type=F32)
            + jnp.dot(lo, m, preferred_element_type=F32))


def _dot_hp(a, b):
    a_hi = a.astype(BF16)
    a_lo = (a - a_hi.astype(F32)).astype(BF16)
    b_hi = b.astype(BF16)
    b_lo = (b - b_hi.astype(F32)).astype(BF16)
    return (jnp.dot(a_hi, b_hi, preferred_element_type=F32) + jnp.dot(a_lo, b_hi, preferred_element_type=F32)
            + jnp.dot(a_hi, b_lo, preferred_element_type=F32))


def _sigmoid(x):
    return 1.0 / (1.0 + jnp.exp(-x))


def _log_sigmoid(x):
    return jnp.minimum(x, 0.0) - jnp.log1p(jnp.exp(-jnp.abs(x)))


def _layer_norm(y, g, b, eps):
    mu = jnp.mean(y, axis=-1, keepdims=True)
    d = y - mu
    var = jnp.mean(d * d, axis=-1, keepdims=True)
    return d * lax.rsqrt(var + eps) * g + b


def _ffn_ln_kernel(x_ref, wg_ref, wu_ref, wd_ref, g_ref, b_ref, o_ref, ob_ref, h_ref, *, alpha, tf):
    x = x_ref[...]
    xb = x.astype(BF16)
    d_ff = wg_ref.shape[1]
    for c in range(d_ff // tf):
        sl = slice(c * tf, (c + 1) * tf)
        gate = jnp.dot(xb, wg_ref[:, sl], preferred_element_type=F32)
        up = jnp.dot(xb, wu_ref[:, sl], preferred_element_type=F32)
        h_ref[:, sl] = (gate * _sigmoid(gate) * up).astype(BF16)
    y = alpha * x + 0.5 * jnp.dot(h_ref[...], wd_ref[...], preferred_element_type=F32)
    out = _layer_norm(y, g_ref[...], b_ref[...], LN_EPS)
    o_ref[...] = out
    ob_ref[...] = out.astype(BF16)


def _ffn_ln(x, wg, wu, wd, g, b, alpha, tm=512, tf=256):
    n, d = x.shape
    d_ff = wg.shape[1]
    const = lambda i: (0, 0)
    return pl.pallas_call(
        functools.partial(_ffn_ln_kernel, alpha=alpha, tf=tf),
        grid=(n // tm,),
        in_specs=[
            pl.BlockSpec((tm, d), lambda i: (i, 0)),
            pl.BlockSpec((d, d_ff), const, pipeline_mode=pl.Buffered(1)),
            pl.BlockSpec((d, d_ff), const, pipeline_mode=pl.Buffered(1)),
            pl.BlockSpec((d_ff, d), const, pipeline_mode=pl.Buffered(1)),
            pl.BlockSpec((1, d), const),
            pl.BlockSpec((1, d), const),
        ],
        out_specs=[pl.BlockSpec((tm, d), lambda i: (i, 0)), pl.BlockSpec((tm, d), lambda i: (i, 0))],
        out_shape=[jax.ShapeDtypeStruct((n, d), F32), jax.ShapeDtypeStruct((n, d), BF16)],
        scratch_shapes=[pltpu.VMEM((tm, d_ff), BF16)],
        compiler_params=pltpu.CompilerParams(dimension_semantics=("arbitrary",), vmem_limit_bytes=VMEM_LIMIT),
        name="ffn_ln",
    )(x, wg, wu, wd, g, b)


def _proj_kernel(x_ref, w_ref, o_ref):
    o_ref[...] = jnp.dot(x_ref[...], w_ref[...], preferred_element_type=F32).astype(o_ref.dtype)


def _proj(xb, w, out_dtype, tm=512):
    n, d = xb.shape
    nc = w.shape[1]
    return pl.pallas_call(
        _proj_kernel,
        grid=(n // tm,),
        in_specs=[pl.BlockSpec((tm, d), lambda i: (i, 0)),
                  pl.BlockSpec((d, nc), lambda i: (0, 0), pipeline_mode=pl.Buffered(1))],
        out_specs=pl.BlockSpec((tm, nc), lambda i: (i, 0)),
        out_shape=jax.ShapeDtypeStruct((n, nc), out_dtype),
        compiler_params=pltpu.CompilerParams(dimension_semantics=("arbitrary",), vmem_limit_bytes=VMEM_LIMIT),
        name="proj",
    )(xb, w)


def _mlstm_kernel(q_ref, k_ref, v_ref, o_ref, g_ref, cw_ref, cb_ref, gb_ref, ng_ref, out_ref,
                  c_ref, n_ref, m_ref, pq_ref, pk_ref, *, nh, dh):
    L = q_ref.shape[0]
    width = nh * dh
    step = pl.program_id(1)

    @pl.when(step == 0)
    def _():
        c_ref[...] = jnp.zeros_like(c_ref)
        n_ref[...] = jnp.zeros_like(n_ref)
        m_ref[...] = jnp.zeros_like(m_ref)
        pq_ref[...] = jnp.zeros_like(pq_ref)
        pk_ref[...] = jnp.zeros_like(pk_ref)

    row = lax.broadcasted_iota(jnp.int32, (L, 1), 0)

    def conv_silu(cur, prev, w_off):
        acc = cb_ref[:, w_off:w_off + width] + cur * cw_ref[M_CONV - 1:M_CONV, w_off:w_off + width]
        for j in range(1, M_CONV):
            shifted = jnp.where(row >= j, pltpu.roll(cur, j, 0), pltpu.roll(prev, j, 0))
            acc = acc + shifted * cw_ref[M_CONV - 1 - j:M_CONV - j, w_off:w_off + width]
        return acc * _sigmoid(acc)

    q_raw = q_ref[...].astype(F32)
    k_raw = k_ref[...].astype(F32)
    q_all = conv_silu(q_raw, pq_ref[...], 0) * (dh ** -0.5)
    k_all = conv_silu(k_raw, pk_ref[...], width)
    pq_ref[...] = q_raw
    pk_ref[...] = k_raw

    gates = g_ref[...] + gb_ref[...]
    log_f = _log_sigmoid(gates)
    ri = lax.broadcasted_iota(jnp.int32, (L, L), 0)
    ci = lax.broadcasted_iota(jnp.int32, (L, L), 1)
    causal = ri >= ci
    b_cols = _dot_exact_lhs(causal.astype(F32), log_f)
    gates_t = gates.T
    b_rows = b_cols.T

    for h in range(nh):
        hs = slice(h * dh, (h + 1) * dh)
        q = q_all[:, hs]
        k = k_all[:, hs]
        v = v_ref[:, hs]
        i_col = gates[:, h:h + 1]
        b_col = b_cols[:, nh + h:nh + h + 1]
        i_row = gates_t[h:h + 1, :]
        b_row = b_rows[nh + h:nh + h + 1, :]
        g_tot = b_col[L - 1:L, :]
        c_prev = c_ref[h]
        n_prev = n_ref[8 * h:8 * h + 1, :]
        m_prev = m_ref[8 * h:8 * h + 1, 0:1]

        a_col = g_tot - b_col + i_col
        m_loc = jnp.max(a_col, axis=0, keepdims=True)
        kw = k * jnp.exp(a_col - m_loc)
        c_loc = _bdot_tn(kw, v)
        n_loc = jnp.sum(kw, axis=0, keepdims=True)

        d_log = jnp.where(causal, b_col - b_row + i_row, -jnp.inf)
        inter = b_col + m_prev
        m_t = jnp.maximum(jnp.max(d_log, axis=-1, keepdims=True), inter)
        s = _bdot_nt(q, k) * jnp.exp(d_log - m_t)
        s_inter = jnp.exp(inter - m_t)
        num = _bdot(s, v) + s_inter * _bdot(q, c_prev)
        den = jnp.sum(s, axis=-1, keepdims=True) + s_inter * jnp.sum(q * n_prev, axis=-1, keepdims=True)
        hid = num / jnp.maximum(jnp.abs(den), jnp.exp(-m_t))

        m_new = jnp.maximum(g_tot + m_prev, m_loc)
        s_old = jnp.exp(g_tot + m_prev - m_new)
        s_new = jnp.exp(m_loc - m_new)
        c_ref[h] = s_old * c_prev + s_new * c_loc
        n_ref[8 * h:8 * h + 1, :] = s_old * n_prev + s_new * n_loc
        m_ref[8 * h:8 * h + 1, :] = jnp.broadcast_to(m_new, (1, m_ref.shape[1]))

        mu = jnp.mean(hid, axis=-1, keepdims=True)
        d = hid - mu
        var = jnp.mean(d * d, axis=-1, keepdims=True)
        hn = d * lax.rsqrt(var + LN_EPS)
        out_ref[:, hs] = (hn * ng_ref[:, hs] * _sigmoid(o_ref[:, hs].astype(F32))).astype(out_ref.dtype)


def _mlstm(pm, pg, conv_w, conv_b, gate_bias, norm_g, bsz, t):
    n = pm.shape[0]
    width = pm.shape[1] // 4
    nh = M_HEADS
    dh = width // nh
    L = M_CHUNK
    nc = t // L
    rows = lambda b, c: b * nc + c
    const = lambda b, c: (0, 0)
    return pl.pallas_call(
        functools.partial(_mlstm_kernel, nh=nh, dh=dh),
        grid=(bsz, nc),
        in_specs=[
            pl.BlockSpec((L, width), lambda b, c: (rows(b, c), 0)),
            pl.BlockSpec((L, width), lambda b, c: (rows(b, c), 1)),
            pl.BlockSpec((L, width), lambda b, c: (rows(b, c), 2)),
            pl.BlockSpec((L, width), lambda b, c: (rows(b, c), 3)),
            pl.BlockSpec((L, 128), lambda b, c: (rows(b, c), 0)),
            pl.BlockSpec((M_CONV, 2 * width), const),
            pl.BlockSpec((1, 2 * width), const),
            pl.BlockSpec((1, 128), const),
            pl.BlockSpec((1, width), const),
        ],
        out_specs=pl.BlockSpec((L, width), lambda b, c: (rows(b, c), 0)),
        out_shape=jax.ShapeDtypeStruct((n, width), BF16),
        scratch_shapes=[
            pltpu.VMEM((nh, dh, dh), F32),
            pltpu.VMEM((8 * nh, dh), F32),
            pltpu.VMEM((8 * nh, 128), F32),
            pltpu.VMEM((L, width), F32),
            pltpu.VMEM((L, width), F32),
        ],
        compiler_params=pltpu.CompilerParams(dimension_semantics=("arbitrary", "arbitrary"),
                                             vmem_limit_bytes=VMEM_LIMIT),
        name="mlstm",
    )(pm, pm, pm, pm, pg, conv_w, conv_b, gate_bias, norm_g)


def _tri_solve_cat(nms, cmm, L):
    shape = nms[0].shape
    ri = lax.broadcasted_iota(jnp.int32, shape, 0)
    ci = lax.broadcasted_iota(jnp.int32, shape, 1) % L
    blk = lambda w: (ri // w) == (ci // w)
    eye = (ri == ci).astype(F32)
    nds = [jnp.where(blk(16), nm, 0.0) for nm in nms]
    ts = [eye + nd for nd in nds]
    p2s = nds
    for _ in range(3):
        p2s = [cmm(p2, p2) for p2 in p2s]
        ts = [t + cmm(p2, t) for t, p2 in zip(ts, p2s)]
    for lo, hi in ((16, 32), (32, 64)):
        off = jnp.logical_and(blk(hi), jnp.logical_not(blk(lo)))
        tns = [cmm(t, jnp.where(off, nm, 0.0)) for t, nm in zip(ts, nms)]
        ts = [t + cmm(tn, t) for t, tn in zip(ts, tns)]
    return ts


def _rwkv_kernel(xr_ref, xk_ref, xv_ref, xl_ref, mur_ref, muk_ref, muv_ref, mul_ref,
                 w0_ref, a0_ref, kkw_ref, kaw_ref, rrk_ref, gng_ref, gnb_ref,
                 w2_ref, a2_ref, g2_ref, out_ref,
                 s_ref, pr_ref, pk_ref, pv_ref, pl_ref, *, L, nchunk, hd, gw):
    R = xr_ref.shape[0]
    width = xr_ref.shape[1]
    ngroups = width // gw
    hpg = gw // hd
    step = pl.program_id(1)

    @pl.when(step == 0)
    def _():
        s_ref[...] = jnp.zeros_like(s_ref)
        pr_ref[...] = jnp.zeros_like(pr_ref)
        pk_ref[...] = jnp.zeros_like(pk_ref)
        pv_ref[...] = jnp.zeros_like(pv_ref)
        pl_ref[...] = jnp.zeros_like(pl_ref)

    row = lax.broadcasted_iota(jnp.int32, (R, 1), 0)

    def shift_mix(x_ref, p_ref, mu_ref, cols):
        x = x_ref[:, cols]
        prev = jnp.where(row == 0, p_ref[0:1, cols], pltpu.roll(x, 1, 0))
        p_ref[0:1, cols] = x[R - 1:R, :]
        return x + (prev - x) * mu_ref[:, cols]

    xl = shift_mix(xl_ref, pl_ref, mul_ref, slice(None))
    xl_tanh = jnp.tanh(xl)
    xl_sig = _sigmoid(xl)

    li = lax.broadcasted_iota(jnp.int32, (gw, gw), 0)
    lj = lax.broadcasted_iota(jnp.int32, (gw, gw), 1)
    same_head = (li // hd) == (lj // hd)
    head_ones = same_head.astype(F32)

    def seg_sum(z):
        return _dot_exact_rhs(z, head_ones)

    def stack(z):
        zb = z.astype(BF16)
        return jnp.where(same_head, jnp.concatenate([zb] * hpg, axis=0), jnp.zeros((), BF16))

    def cmm(a_cat, b):
        return jnp.dot(a_cat.astype(BF16), stack(b), preferred_element_type=F32)

    ti = lax.broadcasted_iota(jnp.int32, (R, R), 0)
    tj = lax.broadcasted_iota(jnp.int32, (R, R), 1)
    chunk_tri = jnp.logical_and(ti // L == tj // L, ti >= tj).astype(F32)
    ci_ = lax.broadcasted_iota(jnp.int32, (L, gw), 0)
    cj_ = lax.broadcasted_iota(jnp.int32, (L, gw), 1) % L
    strict = ci_ > cj_
    incl = ci_ >= cj_

    groups = range(ngroups)
    cols = [slice(grp * gw, (grp + 1) * gw) for grp in groups]
    rr = [shift_mix(xr_ref, pr_ref, mur_ref, cs) for cs in cols]
    rk = [shift_mix(xk_ref, pk_ref, muk_ref, cs) for cs in cols]
    rv = [shift_mix(xv_ref, pv_ref, muv_ref, cs) for cs in cols]

    wl = [w0_ref[:, cs] + _dot_hp(xl_tanh, w2_ref[:, cs]) for cs in cols]
    lw = [-jnp.exp(_log_sigmoid(z) - 0.5) for z in wl]
    a = [_sigmoid(a0_ref[:, cs] + _bdot(xl, a2_ref[:, cs])) for cs in cols]
    gate = [_bdot(xl_sig, g2_ref[:, cs]) for cs in cols]
    kk = [x * kkw_ref[:, cs] for x, cs in zip(rk, cols)]
    kk = [x / jnp.maximum(jnp.sqrt(seg_sum(x * x)), 1e-12) for x in kk]
    k2 = [x * (1.0 + (ai - 1.0) * kaw_ref[:, cs]) for x, ai, cs in zip(rk, a, cols)]
    bv = [x * ai for x, ai in zip(kk, a)]
    bonus = [seg_sum(r * k * rrk_ref[:, cs]) * v for r, k, v, cs in zip(rr, k2, rv, cols)]
    cum = [_dot_exact_lhs(chunk_tri, z) for z in lw]

    probs = []
    for c in range(nchunk):
        rs = slice(c * L, (c + 1) * L)
        for grp in groups:
            cum_c = cum[grp][rs]
            cum_l = cum_c[L - 1:L, :]
            e_neg = jnp.exp(-cum_c)
            e_end = jnp.exp(cum_l - cum_c)
            bv_c = bv[grp][rs]
            k2_c = k2[grp][rs]
            at = -kk[grp][rs] * jnp.exp(cum_c - lw[grp][rs])
            rt = rr[grp][rs] * jnp.exp(cum_c)
            ar = jnp.concatenate([at, rt], axis=0)
            bk_stack = jnp.concatenate([stack(bv_c * e_neg), stack(k2_c * e_neg)], axis=0)
            aa = _bdot_nt(ar, bk_stack)
            probs.append(dict(
                rs=rs, grp=grp, ar=ar, v=rv[grp][rs], w_l=jnp.exp(cum_l),
                bk_end=jnp.concatenate([bv_c * e_end, k2_c * e_end], axis=0),
                nm=jnp.where(strict, aa[:L, :gw], 0.0),
                ak=jnp.concatenate([jnp.where(strict, aa[:L, gw:], 0.0), jnp.where(incl, aa[L:, gw:], 0.0)], axis=0),
                arb=jnp.where(incl, aa[L:, :gw], 0.0)))
    ts = _tri_solve_cat([p["nm"] for p in probs], cmm, L)
    akvs = [cmm(p["ak"], p["v"]) for p in probs]

    for c in range(nchunk):
        idx = [c * ngroups + grp for grp in groups]
        s0 = [s_ref[grp] for grp in groups]
        ps = [_bdot_nt(probs[i]["ar"], s) for i, s in zip(idx, s0)]
        u = [cmm(ts[i], p[:L] + akvs[i][:L]) for i, p in zip(idx, ps)]
        y = [p[L:] + cmm(probs[i]["arb"], ui) + akvs[i][L:] for i, p, ui in zip(idx, ps, u)]
        upd = [_bdot_tn(jnp.concatenate([ui, probs[i]["v"]], axis=0), probs[i]["bk_end"]) for i, ui in zip(idx, u)]
        for grp in groups:
            s_ref[grp] = s0[grp] * probs[idx[grp]]["w_l"] + jnp.where(same_head, upd[grp], 0.0)

        mu = [seg_sum(z) * (1.0 / hd) for z in y]
        d = [z - m for z, m in zip(y, mu)]
        var = [seg_sum(z * z) * (1.0 / hd) for z in d]
        rs = probs[idx[0]]["rs"]
        for grp in groups:
            cs = cols[grp]
            yn = d[grp] * lax.rsqrt(var[grp] + R_GN_EPS) * gng_ref[:, cs] + gnb_ref[:, cs]
            out_ref[rs, cs] = ((yn + bonus[grp][rs]) * gate[grp][rs]).astype(out_ref.dtype)


def _rwkv(pr, r_mu, w0, a0, kkw, kaw, rrk, gng, gnb, w2p, a2p, g2p, bsz, t):
    n = pr.shape[0]
    width = w0.shape[1]
    hd = R_HEAD_DIM
    gw = R_GROUP_HEADS * hd
    L = R_CHUNK
    R = L * R_CHUNKS_PER_STEP
    ns = t // R
    lr = w2p.shape[0]
    lblk = (3 * width) // lr
    rows = lambda b, c: b * ns + c
    xspec = lambda off: pl.BlockSpec((R, width), lambda b, c: (rows(b, c), off))
    pspec = lambda off: pl.BlockSpec((1, width), lambda b, c: (0, off))
    gspec = pl.BlockSpec((1, width), lambda b, c: (0, 0))
    wspec = pl.BlockSpec((lr, width), lambda b, c: (0, 0))
    return pl.pallas_call(
        functools.partial(_rwkv_kernel, L=L, nchunk=R_CHUNKS_PER_STEP, hd=hd, gw=gw),
        grid=(bsz, ns),
        in_specs=[
            xspec(0), xspec(1), xspec(2),
            pl.BlockSpec((R, lr), lambda b, c: (rows(b, c), lblk)),
            pspec(0), pspec(1), pspec(2),
            pl.BlockSpec((1, lr), lambda b, c: (0, lblk)),
            gspec, gspec, gspec, gspec, gspec, gspec, gspec,
            wspec, wspec, wspec,
        ],
        out_specs=pl.BlockSpec((R, width), lambda b, c: (rows(b, c), 0)),
        out_shape=jax.ShapeDtypeStruct((n, width), BF16),
        scratch_shapes=[
            pltpu.VMEM((width // gw, gw, gw), F32),
            pltpu.VMEM((8, width), F32),
            pltpu.VMEM((8, width), F32),
            pltpu.VMEM((8, width), F32),
            pltpu.VMEM((8, lr), F32),
        ],
        compiler_params=pltpu.CompilerParams(dimension_semantics=("arbitrary", "arbitrary"),
                                             vmem_limit_bytes=VMEM_LIMIT),
        name="rwkv",
    )(pr, pr, pr, pr, r_mu, r_mu, r_mu, r_mu, w0, a0, kkw, kaw, rrk, gng, gnb, w2p, a2p, g2p)


def _post_kernel(hm_ref, yr_ref, ga_ref, gb_ref, x_ref, wa_ref, wb_ref, wo_ref, g_ref, b_ref, o_ref, *, alpha):
    ya = jnp.dot(hm_ref[...], wa_ref[...], preferred_element_type=F32)
    yb = jnp.dot(yr_ref[...], wb_ref[...], preferred_element_type=F32)
    merged = _sigmoid(ga_ref[...].astype(F32)) * ya + _sigmoid(gb_ref[...].astype(F32)) * yb
    mix = jnp.dot(merged.astype(BF16), wo_ref[...], preferred_element_type=F32)
    o_ref[...] = _layer_norm(alpha * x_ref[...] + mix, g_ref[...], b_ref[...], LN_EPS)


def _post(hm, yr, pgate, x1, wa, wb, wo, g, b, alpha, tm=512):
    n, d = x1.shape
    tile = lambda j: pl.BlockSpec((tm, d), lambda i: (i, j))
    wspec = pl.BlockSpec((d, d), lambda i: (0, 0), pipeline_mode=pl.Buffered(1))
    vspec = pl.BlockSpec((1, d), lambda i: (0, 0))
    return pl.pallas_call(
        functools.partial(_post_kernel, alpha=alpha),
        grid=(n // tm,),
        in_specs=[tile(0), tile(0), tile(0), tile(1), tile(0), wspec, wspec, wspec, vspec, vspec],
        out_specs=tile(0),
        out_shape=jax.ShapeDtypeStruct((n, d), F32),
        compiler_params=pltpu.CompilerParams(dimension_semantics=("arbitrary",), vmem_limit_bytes=VMEM_LIMIT),
        name="post",
    )(hm, yr, pgate, pgate, x1, wa, wb, wo, g, b)


def kernel(x, ffn1_w_gate, ffn1_w_up, ffn1_w_down, ln1_g, ln1_b, w_in, m_conv_w, m_conv_b, m_i_bias, m_f_bias,
           m_norm_g, r_mu, r_w0, r_w2, r_a0, r_a2, r_g2, r_k_k, r_k_a, r_r_k, r_gn_g, r_gn_b, w_branch_a,
           w_branch_b, w_out, ln2_g, ln2_b, ffn2_w_gate, ffn2_w_up, ffn2_w_down, ln3_g, ln3_b):
    bsz, t, d = x.shape
    depth = w_in.shape[0]
    alpha = (2 * depth) ** 0.25
    mw = d
    rw = d
    nh = M_HEADS
    lr = R_DECAY_RANK + R_AAA_RANK + R_GATE_RANK
    row = lambda p: p.reshape(1, -1)

    cur = x.reshape(bsz * t, d)
    for l in range(depth):
        x1, x1b = _ffn_ln(cur, ffn1_w_gate[l].astype(BF16), ffn1_w_up[l].astype(BF16),
                          ffn1_w_down[l].astype(BF16), row(ln1_g[l]), row(ln1_b[l]), alpha)

        w = w_in[l]
        o_gate = 4 * mw
        o_rc = o_gate + 2 * nh
        o_ga = o_rc + 3 * rw + lr
        w_m = w[:, :o_gate].astype(BF16)
        w_g = jnp.pad(w[:, o_gate:o_rc], ((0, 0), (0, 128 - 2 * nh))).astype(BF16)
        w_r = w[:, o_rc:o_ga].astype(BF16)
        w_gate = w[:, o_ga:].astype(BF16)
        pm = _proj(x1b, w_m, BF16)
        pg = _proj(x1b, w_g, F32)
        pr = _proj(x1b, w_r, F32)
        pgate = _proj(x1b, w_gate, BF16)

        gate_bias = jnp.pad(jnp.concatenate([m_i_bias[l], m_f_bias[l]]), (0, 128 - 2 * nh)).reshape(1, 128)
        hm = _mlstm(pm, pg, m_conv_w[l], row(m_conv_b[l]), gate_bias, row(m_norm_g[l]), bsz, t)

        zeros = lambda r: jnp.zeros((r, rw), F32)
        w2p = jnp.concatenate([r_w2[l], zeros(R_AAA_RANK + R_GATE_RANK)], axis=0)
        a2p = jnp.concatenate([zeros(R_DECAY_RANK), r_a2[l], zeros(R_GATE_RANK)], axis=0)
        g2p = jnp.concatenate([zeros(R_DECAY_RANK + R_AAA_RANK), r_g2[l]], axis=0)
        yr = _rwkv(pr, row(r_mu[l]), row(r_w0[l]), row(r_a0[l]), row(r_k_k[l]), row(r_k_a[l]), row(r_r_k[l]),
                   row(r_gn_g[l]), row(r_gn_b[l]), w2p, a2p, g2p, bsz, t)

        x2 = _post(hm, yr, pgate, x1, w_branch_a[l].astype(BF16), w_branch_b[l].astype(BF16),
                   w_out[l].astype(BF16), row(ln2_g[l]), row(ln2_b[l]), alpha)
        cur, _ = _ffn_ln(x2, ffn2_w_gate[l].astype(BF16), ffn2_w_up[l].astype(BF16),
                         ffn2_w_down[l].astype(BF16), row(ln3_g[l]), row(ln3_b[l]), alpha)
    return cur.reshape(bsz, t, d)
```

```python
import functools

import jax
import jax.numpy as jnp
from jax import lax
from jax.experimental import pallas as pl
from jax.experimental.pallas import tpu as pltpu

F32 = jnp.float32
BF16 = jnp.bfloat16

M_HEADS = 4
M_CONV = 4
M_CHUNK = 128
M_HALO = 16
R_HEAD_DIM = 64
R_DECAY_RANK = 64
R_AAA_RANK = 64
R_GATE_RANK = 128
R_GN_EPS = 64e-5
LN_EPS = 1e-5

R_CHUNK = 64
R_GROUP_HEADS = 4
R_CHUNKS_PER_STEP = 2
VMEM_LIMIT = 56 * 1024 * 1024


def _bdot(a, b):
    return jnp.dot(a.astype(BF16), b.astype(BF16), preferred_element_type=F32)


def _bdot_nt(a, b):
    return lax.dot_general(a.astype(BF16), b.astype(BF16), (((1,), (1,)), ((), ())),
                           preferred_element_type=F32)


def _bdot_tn(a, b):
    return lax.dot_general(a.astype(BF16), b.astype(BF16), (((0,), (0,)), ((), ())),
                           preferred_element_type=F32)


def _split3(x):
    hi = x.astype(BF16)
    r1 = x - hi.astype(F32)
    mid = r1.astype(BF16)
    lo = (r1 - mid.astype(F32)).astype(BF16)
    return hi, mid, lo


def _dot_exact_lhs(m01, x):
    m = m01.astype(BF16)
    hi, mid, lo = _split3(x)
    return (jnp.dot(m, hi, preferred_element_type=F32) + jnp.dot(m, mid, preferred_element_type=F32)
            + jnp.dot(m, lo, preferred_element_type=F32))


def _dot_exact_rhs(x, m01):
    m = m01.astype(BF16)
    hi, mid, lo = _split3(x)
    return (jnp.dot(hi, m, preferred_element_type=F32) + jnp.dot(mid, m, preferred_element_type=F32)
            + jnp.dot(lo, m, preferred_element_type=F32))


def _dot_hp(a, b):
    a_hi = a.astype(BF16)
    a_lo = (a - a_hi.astype(F32)).astype(BF16)
    b_hi = b.astype(BF16)
    b_lo = (b - b_hi.astype(F32)).astype(BF16)
    return (jnp.dot(a_hi, b_hi, preferred_element_type=F32) + jnp.dot(a_lo, b_hi, preferred_element_type=F32)
            + jnp.dot(a_hi, b_lo, preferred_element_type=F32))


def _sigmoid(x):
    return 0.5 * jnp.tanh(0.5 * x) + 0.5


def _log_sigmoid(x):
    return jnp.minimum(x, 0.0) - jnp.log1p(jnp.exp(-jnp.abs(x)))


def _layer_norm(y, g, b, eps):
    mu = jnp.mean(y, axis=-1, keepdims=True)
    d = y - mu
    var = jnp.mean(d * d, axis=-1, keepdims=True)
    return d * lax.rsqrt(var + eps) * g + b


def _ffn_ln_kernel(x_ref, wg_ref, wu_ref, wd_ref, g_ref, b_ref, o_ref, ob_ref, h_ref, *, alpha, tf):
    x = x_ref[...]
    xb = x.astype(BF16)
    d_ff = wg_ref.shape[1]
    for c in range(d_ff // tf):
        sl = slice(c * tf, (c + 1) * tf)
        gate = jnp.dot(xb, wg_ref[:, sl], preferred_element_type=F32)
        up = jnp.dot(xb, wu_ref[:, sl], preferred_element_type=F32)
        h_ref[:, sl] = (gate * _sigmoid(gate) * up).astype(BF16)
    y = alpha * x + 0.5 * jnp.dot(h_ref[...], wd_ref[...], preferred_element_type=F32)
    out = _layer_norm(y, g_ref[...], b_ref[...], LN_EPS)
    o_ref[...] = out
    ob_ref[...] = out.astype(BF16)


def _ffn_ln(x, wg, wu, wd, g, b, alpha, tm=512, tf=256):
    n, d = x.shape
    d_ff = wg.shape[1]
    const = lambda i: (0, 0)
    return pl.pallas_call(
        functools.partial(_ffn_ln_kernel, alpha=alpha, tf=tf),
        grid=(n // tm,),
        in_specs=[
            pl.BlockSpec((tm, d), lambda i: (i, 0)),
            pl.BlockSpec((d, d_ff), const, pipeline_mode=pl.Buffered(1)),
            pl.BlockSpec((d, d_ff), const, pipeline_mode=pl.Buffered(1)),
            pl.BlockSpec((d_ff, d), const, pipeline_mode=pl.Buffered(1)),
            pl.BlockSpec((1, d), const),
            pl.BlockSpec((1, d), const),
        ],
        out_specs=[pl.BlockSpec((tm, d), lambda i: (i, 0)), pl.BlockSpec((tm, d), lambda i: (i, 0))],
        out_shape=[jax.ShapeDtypeStruct((n, d), F32), jax.ShapeDtypeStruct((n, d), BF16)],
        scratch_shapes=[pltpu.VMEM((tm, d_ff), BF16)],
        compiler_params=pltpu.CompilerParams(dimension_semantics=("arbitrary",), vmem_limit_bytes=VMEM_LIMIT),
        name="ffn_ln",
    )(x, wg, wu, wd, g, b)


def _proj_kernel(x_ref, w_ref, o_ref):
    o_ref[...] = jnp.dot(x_ref[...], w_ref[...], preferred_element_type=F32).astype(o_ref.dtype)


def _proj(xb, w, out_dtype, tm=512):
    n, d = xb.shape
    nc = w.shape[1]
    return pl.pallas_call(
        _proj_kernel,
        grid=(n // tm,),
        in_specs=[pl.BlockSpec((tm, d), lambda i: (i, 0)),
                  pl.BlockSpec((d, nc), lambda i: (0, 0), pipeline_mode=pl.Buffered(1))],
        out_specs=pl.BlockSpec((tm, nc), lambda i: (i, 0)),
        out_shape=jax.ShapeDtypeStruct((n, nc), out_dtype),
        compiler_params=pltpu.CompilerParams(dimension_semantics=("arbitrary",), vmem_limit_bytes=VMEM_LIMIT),
        name="proj",
    )(xb, w)


def _mlstm_kernel(q_ref, k_ref, v_ref, o_ref, g_ref, cw_ref, cb_ref, gb_ref, ng_ref, out_ref,
                  c_ref, n_ref, m_ref, pq_ref, pk_ref, *, nh, dh):
    L = q_ref.shape[0]
    width = nh * dh
    halo = pq_ref.shape[0]
    step = pl.program_id(1)

    @pl.when(step == 0)
    def _():
        c_ref[...] = jnp.zeros_like(c_ref)
        n_ref[...] = jnp.zeros_like(n_ref)
        m_ref[...] = jnp.zeros_like(m_ref)
        pq_ref[...] = jnp.zeros_like(pq_ref)
        pk_ref[...] = jnp.zeros_like(pk_ref)

    sr = lax.broadcasted_iota(jnp.int32, ((M_CONV - 1) * L, halo + L), 0)
    sc = lax.broadcasted_iota(jnp.int32, ((M_CONV - 1) * L, halo + L), 1)
    shift = (sc == halo + sr % L - (sr // L + 1)).astype(BF16)

    def conv_silu(x_ref, p_ref, w_off):
        cur = x_ref[...]
        delayed = jnp.dot(shift, jnp.concatenate([p_ref[...], cur], axis=0), preferred_element_type=F32)
        p_ref[...] = cur[L - halo:, :]
        acc = cb_ref[:, w_off:w_off + width] + cur.astype(F32) * cw_ref[M_CONV - 1:M_CONV, w_off:w_off + width]
        for j in range(1, M_CONV):
            acc = acc + delayed[(j - 1) * L:j * L] * cw_ref[M_CONV - 1 - j:M_CONV - j, w_off:w_off + width]
        return acc * _sigmoid(acc)

    q_all = conv_silu(q_ref, pq_ref, 0) * (dh ** -0.5)
    k_all = conv_silu(k_ref, pk_ref, width)

    gates = g_ref[...] + gb_ref[...]
    log_f = _log_sigmoid(gates)
    ri = lax.broadcasted_iota(jnp.int32, (L, L), 0)
    ci = lax.broadcasted_iota(jnp.int32, (L, L), 1)
    causal = ri >= ci
    b_cols = _dot_exact_lhs(causal.astype(F32), log_f)
    gates_t = gates.T
    b_rows = b_cols.T

    heads = range(nh)
    hs = [slice(h * dh, (h + 1) * dh) for h in heads]
    q = [q_all[:, s] for s in hs]
    k = [k_all[:, s] for s in hs]
    v = [v_ref[:, s] for s in hs]
    i_col = [gates[:, h:h + 1] for h in heads]
    b_col = [b_cols[:, nh + h:nh + h + 1] for h in heads]
    i_row = [gates_t[h:h + 1, :] for h in heads]
    b_row = [b_rows[nh + h:nh + h + 1, :] for h in heads]
    g_tot = [b[L - 1:L, :] for b in b_col]
    c_prev = [c_ref[h] for h in heads]
    n_prev = [n_ref[8 * h:8 * h + 1, :] for h in heads]
    m_prev = [m_ref[8 * h:8 * h + 1, 0:1] for h in heads]

    a_col = [g_tot[h] - b_col[h] + i_col[h] for h in heads]
    m_loc = [jnp.max(z, axis=0, keepdims=True) for z in a_col]
    kw = [k[h] * jnp.exp(a_col[h] - m_loc[h]) for h in heads]
    c_loc = [_bdot_tn(kw[h], v[h]) for h in heads]
    n_loc = [jnp.sum(z, axis=0, keepdims=True) for z in kw]

    d_log = [jnp.where(causal, b_col[h] - b_row[h] + i_row[h], -jnp.inf) for h in heads]
    inter = [b_col[h] + m_prev[h] for h in heads]
    m_t = [jnp.maximum(jnp.max(d_log[h], axis=-1, keepdims=True), inter[h]) for h in heads]
    s = [_bdot_nt(q[h], k[h]) * jnp.exp(d_log[h] - m_t[h]) for h in heads]
    s_inter = [jnp.exp(inter[h] - m_t[h]) for h in heads]
    num = [_bdot(s[h], v[h]) + s_inter[h] * _bdot(q[h], c_prev[h]) for h in heads]
    den = [jnp.sum(s[h], axis=-1, keepdims=True) + s_inter[h] * jnp.sum(q[h] * n_prev[h], axis=-1, keepdims=True)
           for h in heads]
    hid = [num[h] / jnp.maximum(jnp.abs(den[h]), jnp.exp(-m_t[h])) for h in heads]

    m_new = [jnp.maximum(g_tot[h] + m_prev[h], m_loc[h]) for h in heads]
    for h in heads:
        s_old = jnp.exp(g_tot[h] + m_prev[h] - m_new[h])
        s_new = jnp.exp(m_loc[h] - m_new[h])
        c_ref[h] = s_old * c_prev[h] + s_new * c_loc[h]
        n_ref[8 * h:8 * h + 1, :] = s_old * n_prev[h] + s_new * n_loc[h]
        m_ref[8 * h:8 * h + 1, :] = jnp.broadcast_to(m_new[h], (1, m_ref.shape[1]))

    for h in heads:
        mu = jnp.mean(hid[h], axis=-1, keepdims=True)
        d = hid[h] - mu
        var = jnp.mean(d * d, axis=-1, keepdims=True)
        hn = d * lax.rsqrt(var + LN_EPS)
        out_ref[:, hs[h]] = (hn * ng_ref[:, hs[h]] * _sigmoid(o_ref[:, hs[h]].astype(F32))).astype(out_ref.dtype)


def _mlstm(pm, pg, conv_w, conv_b, gate_bias, norm_g, bsz, t):
    n = pm.shape[0]
    width = pm.shape[1] // 4
    nh = M_HEADS
    dh = width // nh
    L = M_CHUNK
    nc = t // L
    rows = lambda b, c: b * nc + c
    const = lambda b, c: (0, 0)
    return pl.pallas_call(
        functools.partial(_mlstm_kernel, nh=nh, dh=dh),
        grid=(bsz, nc),
        in_specs=[
            pl.BlockSpec((L, width), lambda b, c: (rows(b, c), 0)),
            pl.BlockSpec((L, width), lambda b, c: (rows(b, c), 1)),
            pl.BlockSpec((L, width), lambda b, c: (rows(b, c), 2)),
            pl.BlockSpec((L, width), lambda b, c: (rows(b, c), 3)),
            pl.BlockSpec((L, 128), lambda b, c: (rows(b, c), 0)),
            pl.BlockSpec((M_CONV, 2 * width), const),
            pl.BlockSpec((1, 2 * width), const),
            pl.BlockSpec((1, 128), const),
            pl.BlockSpec((1, width), const),
        ],
        out_specs=pl.BlockSpec((L, width), lambda b, c: (rows(b, c), 0)),
        out_shape=jax.ShapeDtypeStruct((n, width), BF16),
        scratch_shapes=[
            pltpu.VMEM((nh, dh, dh), F32),
            pltpu.VMEM((8 * nh, dh), F32),
            pltpu.VMEM((8 * nh, 128), F32),
            pltpu.VMEM((M_HALO, width), BF16),
            pltpu.VMEM((M_HALO, width), BF16),
        ],
        compiler_params=pltpu.CompilerParams(dimension_semantics=("arbitrary", "arbitrary"),
                                             vmem_limit_bytes=VMEM_LIMIT),
        name="mlstm",
    )(pm, pm, pm, pm, pg, conv_w, conv_b, gate_bias, norm_g)


def _tri_solve_cat(nms, cmm, L):
    shape = nms[0].shape
    ri = lax.broadcasted_iota(jnp.int32, shape, 0)
    ci = lax.broadcasted_iota(jnp.int32, shape, 1) % L
    blk = lambda w: (ri // w) == (ci // w)
    eye = (ri == ci).astype(F32)
    nds = [jnp.where(blk(16), nm, 0.0) for nm in nms]
    ts = [eye + nd for nd in nds]
    p2s = nds
    for _ in range(3):
        p2s = [cmm(p2, p2) for p2 in p2s]
        ts = [t + cmm(p2, t) for t, p2 in zip(ts, p2s)]
    for lo, hi in ((16, 32), (32, 64)):
        off = jnp.logical_and(blk(hi), jnp.logical_not(blk(lo)))
        tns = [cmm(t, jnp.where(off, nm, 0.0)) for t, nm in zip(ts, nms)]
        ts = [t + cmm(tn, t) for t, tn in zip(ts, tns)]
    return ts


def _rwkv_kernel(xr_ref, xk_ref, xv_ref, xl_ref, mur_ref, muk_ref, muv_ref, mul_ref,
                 w0_ref, a0_ref, kkw_ref, kaw_ref, rrk_ref, gng_ref, gnb_ref,
                 w2_ref, a2_ref, g2_ref, out_ref,
                 s_ref, pr_ref, pk_ref, pv_ref, pl_ref, *, L, nchunk, hd, gw):
    R = xr_ref.shape[0]
    width = xr_ref.shape[1]
    ngroups = width // gw
    hpg = gw // hd
    step = pl.program_id(1)

    @pl.when(step == 0)
    def _():
        s_ref[...] = jnp.zeros_like(s_ref)
        pr_ref[...] = jnp.zeros_like(pr_ref)
        pk_ref[...] = jnp.zeros_like(pk_ref)
        pv_ref[...] = jnp.zeros_like(pv_ref)
        pl_ref[...] = jnp.zeros_like(pl_ref)

    row = lax.broadcasted_iota(jnp.int32, (R, 1), 0)

    def shift_mix(x_ref, p_ref, mu_ref, cols):
        x = x_ref[:, cols]
        prev = jnp.where(row == 0, p_ref[0:1, cols], pltpu.roll(x, 1, 0))
        p_ref[0:1, cols] = x[R - 1:R, :]
        return x + (prev - x) * mu_ref[:, cols]

    xl = shift_mix(xl_ref, pl_ref, mul_ref, slice(None))
    xl_tanh = jnp.tanh(xl)
    xl_sig = _sigmoid(xl)

    li = lax.broadcasted_iota(jnp.int32, (gw, gw), 0)
    lj = lax.broadcasted_iota(jnp.int32, (gw, gw), 1)
    same_head = (li // hd) == (lj // hd)
    head_ones = same_head.astype(BF16)

    def seg_sums(zs):
        hi = [z.astype(BF16) for z in zs]
        lo = [(z - h.astype(F32)).astype(BF16) for z, h in zip(zs, hi)]
        sums = jnp.dot(jnp.concatenate(hi + lo, axis=0), head_ones, preferred_element_type=F32)
        n, rows = len(zs), zs[0].shape[0]
        return [sums[i * rows:(i + 1) * rows] + sums[(n + i) * rows:(n + i + 1) * rows] for i in range(n)]

    def stack(z):
        zb = z.astype(BF16)
        return jnp.where(same_head, jnp.concatenate([zb] * hpg, axis=0), jnp.zeros((), BF16))

    def cmm(a_cat, b):
        return jnp.dot(a_cat.astype(BF16), stack(b), preferred_element_type=F32)

    ti = lax.broadcasted_iota(jnp.int32, (R, R), 0)
    tj = lax.broadcasted_iota(jnp.int32, (R, R), 1)
    chunk_tri = jnp.logical_and(ti // L == tj // L, ti >= tj).astype(F32)
    ci_ = lax.broadcasted_iota(jnp.int32, (L, gw), 0)
    cj_ = lax.broadcasted_iota(jnp.int32, (L, gw), 1) % L
    strict = ci_ > cj_
    incl = ci_ >= cj_

    groups = range(ngroups)
    cols = [slice(grp * gw, (grp + 1) * gw) for grp in groups]
    rr = [shift_mix(xr_ref, pr_ref, mur_ref, cs) for cs in cols]
    rk = [shift_mix(xk_ref, pk_ref, muk_ref, cs) for cs in cols]
    rv = [shift_mix(xv_ref, pv_ref, muv_ref, cs) for cs in cols]

    wl = [w0_ref[:, cs] + _dot_hp(xl_tanh, w2_ref[:, cs]) for cs in cols]
    lw = [-jnp.exp(_log_sigmoid(z) - 0.5) for z in wl]
    a = [_sigmoid(a0_ref[:, cs] + _bdot(xl, a2_ref[:, cs])) for cs in cols]
    gate = [_bdot(xl_sig, g2_ref[:, cs]) for cs in cols]
    kk = [x * kkw_ref[:, cs] for x, cs in zip(rk, cols)]
    k2 = [x * (1.0 + (ai - 1.0) * kaw_ref[:, cs]) for x, ai, cs in zip(rk, a, cols)]
    sums = seg_sums([x * x for x in kk] + [r * k * rrk_ref[:, cs] for r, k, cs in zip(rr, k2, cols)])
    kk = [x / jnp.maximum(jnp.sqrt(s), 1e-12) for x, s in zip(kk, sums[:ngroups])]
    bv = [x * ai for x, ai in zip(kk, a)]
    bonus = [s * v for s, v in zip(sums[ngroups:], rv)]
    cum = [_dot_exact_lhs(chunk_tri, z) for z in lw]

    probs = []
    for c in range(nchunk):
        rs = slice(c * L, (c + 1) * L)
        for grp in groups:
            cum_c = cum[grp][rs]
            cum_l = cum_c[L - 1:L, :]
            e_neg = jnp.exp(-cum_c)
            e_end = jnp.exp(cum_l - cum_c)
            bv_c = bv[grp][rs]
            k2_c = k2[grp][rs]
            at = -kk[grp][rs] * jnp.exp(cum_c - lw[grp][rs])
            rt = rr[grp][rs] * jnp.exp(cum_c)
            ar = jnp.concatenate([at, rt], axis=0)
            bk_stack = jnp.concatenate([stack(bv_c * e_neg), stack(k2_c * e_neg)], axis=0)
            aa = _bdot_nt(ar, bk_stack)
            probs.append(dict(
                rs=rs, grp=grp, ar=ar, v=rv[grp][rs], w_l=jnp.exp(cum_l),
                bk_end=jnp.concatenate([bv_c * e_end, k2_c * e_end], axis=0),
                nm=jnp.where(strict, aa[:L, :gw], 0.0),
                ak=jnp.concatenate([jnp.where(strict, aa[:L, gw:], 0.0), jnp.where(incl, aa[L:, gw:], 0.0)], axis=0),
                arb=jnp.where(incl, aa[L:, :gw], 0.0)))
    ts = _tri_solve_cat([p["nm"] for p in probs], cmm, L)
    akvs = [cmm(p["ak"], p["v"]) for p in probs]

    for c in range(nchunk):
        idx = [c * ngroups + grp for grp in groups]
        s0 = [s_ref[grp] for grp in groups]
        ps = [_bdot_nt(probs[i]["ar"], s) for i, s in zip(idx, s0)]
        u = [cmm(ts[i], p[:L] + akvs[i][:L]) for i, p in zip(idx, ps)]
        y = [p[L:] + cmm(probs[i]["arb"], ui) + akvs[i][L:] for i, p, ui in zip(idx, ps, u)]
        upd = [_bdot_tn(jnp.concatenate([ui, probs[i]["v"]], axis=0), probs[i]["bk_end"]) for i, ui in zip(idx, u)]
        for grp in groups:
            s_ref[grp] = s0[grp] * probs[idx[grp]]["w_l"] + jnp.where(same_head, upd[grp], 0.0)

        mu = [z * (1.0 / hd) for z in seg_sums(y)]
        d = [z - m for z, m in zip(y, mu)]
        var = [z * (1.0 / hd) for z in seg_sums([z * z for z in d])]
        rs = probs[idx[0]]["rs"]
        for grp in groups:
            cs = cols[grp]
            yn = d[grp] * lax.rsqrt(var[grp] + R_GN_EPS) * gng_ref[:, cs] + gnb_ref[:, cs]
            out_ref[rs, cs] = ((yn + bonus[grp][rs]) * gate[grp][rs]).astype(out_ref.dtype)


def _rwkv(pr, r_mu, w0, a0, kkw, kaw, rrk, gng, gnb, w2p, a2p, g2p, bsz, t):
    n = pr.shape[0]
    width = w0.shape[1]
    hd = R_HEAD_DIM
    gw = R_GROUP_HEADS * hd
    L = R_CHUNK
    R = L * R_CHUNKS_PER_STEP
    ns = t // R
    lr = w2p.shape[0]
    lblk = (3 * width) // lr
    rows = lambda b, c: b * ns + c
    xspec = lambda off: pl.BlockSpec((R, width), lambda b, c: (rows(b, c), off))
    pspec = lambda off: pl.BlockSpec((1, width), lambda b, c: (0, off))
    gspec = pl.BlockSpec((1, width), lambda b, c: (0, 0))
    wspec = pl.BlockSpec((lr, width), lambda b, c: (0, 0))
    return pl.pallas_call(
        functools.partial(_rwkv_kernel, L=L, nchunk=R_CHUNKS_PER_STEP, hd=hd, gw=gw),
        grid=(bsz, ns),
        in_specs=[
            xspec(0), xspec(1), xspec(2),
            pl.BlockSpec((R, lr), lambda b, c: (rows(b, c), lblk)),
            pspec(0), pspec(1), pspec(2),
            pl.BlockSpec((1, lr), lambda b, c: (0, lblk)),
            gspec, gspec, gspec, gspec, gspec, gspec, gspec,
            wspec, wspec, wspec,
        ],
        out_specs=pl.BlockSpec((R, width), lambda b, c: (rows(b, c), 0)),
        out_shape=jax.ShapeDtypeStruct((n, width), BF16),
        scratch_shapes=[
            pltpu.VMEM((width // gw, gw, gw), F32),
            pltpu.VMEM((8, width), F32),
            pltpu.VMEM((8, width), F32),
            pltpu.VMEM((8, width), F32),
            pltpu.VMEM((8, lr), F32),
        ],
        compiler_params=pltpu.CompilerParams(dimension_semantics=("arbitrary", "arbitrary"),
                                             vmem_limit_bytes=VMEM_LIMIT),
        name="rwkv",
    )(pr, pr, pr, pr, r_mu, r_mu, r_mu, r_mu, w0, a0, kkw, kaw, rrk, gng, gnb, w2p, a2p, g2p)


def _post_kernel(hm_ref, yr_ref, ga_ref, gb_ref, x_ref, wa_ref, wb_ref, wo_ref, g_ref, b_ref, o_ref, *, alpha):
    ya = jnp.dot(hm_ref[...], wa_ref[...], preferred_element_type=F32)
    yb = jnp.dot(yr_ref[...], wb_ref[...], preferred_element_type=F32)
    merged = _sigmoid(ga_ref[...].astype(F32)) * ya + _sigmoid(gb_ref[...].astype(F32)) * yb
    mix = jnp.dot(merged.astype(BF16), wo_ref[...], preferred_element_type=F32)
    o_ref[...] = _layer_norm(alpha * x_ref[...] + mix, g_ref[...], b_ref[...], LN_EPS)


def _post(hm, yr, pgate, x1, wa, wb, wo, g, b, alpha, tm=512):
    n, d = x1.shape
    tile = lambda j: pl.BlockSpec((tm, d), lambda i: (i, j))
    wspec = pl.BlockSpec((d, d), lambda i: (0, 0), pipeline_mode=pl.Buffered(1))
    vspec = pl.BlockSpec((1, d), lambda i: (0, 0))
    return pl.pallas_call(
        functools.partial(_post_kernel, alpha=alpha),
        grid=(n // tm,),
        in_specs=[tile(0), tile(0), tile(0), tile(1), tile(0), wspec, wspec, wspec, vspec, vspec],
        out_specs=tile(0),
        out_shape=jax.ShapeDtypeStruct((n, d), F32),
        compiler_params=pltpu.CompilerParams(dimension_semantics=("arbitrary",), vmem_limit_bytes=VMEM_LIMIT),
        name="post",
    )(hm, yr, pgate, pgate, x1, wa, wb, wo, g, b)


def kernel(x, ffn1_w_gate, ffn1_w_up, ffn1_w_down, ln1_g, ln1_b, w_in, m_conv_w, m_conv_b, m_i_bias, m_f_bias,
           m_norm_g, r_mu, r_w0, r_w2, r_a0, r_a2, r_g2, r_k_k, r_k_a, r_r_k, r_gn_g, r_gn_b, w_branch_a,
           w_branch_b, w_out, ln2_g, ln2_b, ffn2_w_gate, ffn2_w_up, ffn2_w_down, ln3_g, ln3_b):
    bsz, t, d = x.shape
    depth = w_in.shape[0]
    alpha = (2 * depth) ** 0.25
    mw = d
    rw = d
    nh = M_HEADS
    lr = R_DECAY_RANK + R_AAA_RANK + R_GATE_RANK
    row = lambda p: p.reshape(1, -1)

    cur = x.reshape(bsz * t, d)
    for l in range(depth):
        x1, x1b = _ffn_ln(cur, ffn1_w_gate[l].astype(BF16), ffn1_w_up[l].astype(BF16),
                          ffn1_w_down[l].astype(BF16), row(ln1_g[l]), row(ln1_b[l]), alpha)

        w = w_in[l]
        o_gate = 4 * mw
        o_rc = o_gate + 2 * nh
        o_ga = o_rc + 3 * rw + lr
        w_m = w[:, :o_gate].astype(BF16)
        w_g = jnp.pad(w[:, o_gate:o_rc], ((0, 0), (0, 128 - 2 * nh))).astype(BF16)
        w_r = w[:, o_rc:o_ga].astype(BF16)
        w_gate = w[:, o_ga:].astype(BF16)
        pm = _proj(x1b, w_m, BF16)
        pg = _proj(x1b, w_g, F32)
        pr = _proj(x1b, w_r, F32)
        pgate = _proj(x1b, w_gate, BF16)

        gate_bias = jnp.pad(jnp.concatenate([m_i_bias[l], m_f_bias[l]]), (0, 128 - 2 * nh)).reshape(1, 128)
        hm = _mlstm(pm, pg, m_conv_w[l], row(m_conv_b[l]), gate_bias, row(m_norm_g[l]), bsz, t)

        zeros = lambda r: jnp.zeros((r, rw), F32)
        w2p = jnp.concatenate([r_w2[l], zeros(R_AAA_RANK + R_GATE_RANK)], axis=0)
        a2p = jnp.concatenate([zeros(R_DECAY_RANK), r_a2[l], zeros(R_GATE_RANK)], axis=0)
        g2p = jnp.concatenate([zeros(R_DECAY_RANK + R_AAA_RANK), r_g2[l]], axis=0)
        yr = _rwkv(pr, row(r_mu[l]), row(r_w0[l]), row(r_a0[l]), row(r_k_k[l]), row(r_k_a[l]), row(r_r_k[l]),
                   row(r_gn_g[l]), row(r_gn_b[l]), w2p, a2p, g2p, bsz, t)

        x2 = _post(hm, yr, pgate, x1, w_branch_a[l].astype(BF16), w_branch_b[l].astype(BF16),
                   w_out[l].astype(BF16), row(ln2_g[l]), row(ln2_b[l]), alpha)
        cur, _ = _ffn_ln(x2, ffn2_w_gate[l].astype(BF16), ffn2_w_up[l].astype(BF16),
                         ffn2_w_down[l].astype(BF16), row(ln3_g[l]), row(ln3_b[l]), alpha)
    return cur.reshape(bsz, t, d)
```

```python
import functools

import jax
import jax.numpy as jnp
from jax import lax
from jax.experimental import pallas as pl
from jax.experimental.pallas import tpu as pltpu

F32 = jnp.float32
BF16 = jnp.bfloat16

M_HEADS = 4
M_CONV = 4
M_CHUNK = 128
M_HALO = 16
R_HEAD_DIM = 64
R_DECAY_RANK = 64
R_AAA_RANK = 64
R_GATE_RANK = 128
R_GN_EPS = 64e-5
LN_EPS = 1e-5

R_CHUNK = 64
R_GROUP_HEADS = 2
R_CHUNKS_PER_STEP = 4
VMEM_LIMIT = 56 * 1024 * 1024


def _bdot(a, b):
    return jnp.dot(a.astype(BF16), b.astype(BF16), preferred_element_type=F32)


def _bdot_nt(a, b):
    return lax.dot_general(a.astype(BF16), b.astype(BF16), (((1,), (1,)), ((), ())),
                           preferred_element_type=F32)


def _bdot_tn(a, b):
    return lax.dot_general(a.astype(BF16), b.astype(BF16), (((0,), (0,)), ((), ())),
                           preferred_element_type=F32)


def _split3(x):
    hi = x.astype(BF16)
    r1 = x - hi.astype(F32)
    mid = r1.astype(BF16)
    lo = (r1 - mid.astype(F32)).astype(BF16)
    return hi, mid, lo


def _dot_exact_lhs(m01, x):
    m = m01.astype(BF16)
    hi, mid, lo = _split3(x)
    return (jnp.dot(m, hi, preferred_element_type=F32) + jnp.dot(m, mid, preferred_element_type=F32)
            + jnp.dot(m, lo, preferred_element_type=F32))


def _dot_exact_rhs(x, m01):
    m = m01.astype(BF16)
    hi, mid, lo = _split3(x)
    return (jnp.dot(hi, m, preferred_element_type=F32) + jnp.dot(mid, m, preferred_element_type=F32)
            + jnp.dot(lo, m, preferred_element_type=F32))


def _dot_hp(a, b):
    a_hi = a.astype(BF16)
    a_lo = (a - a_hi.astype(F32)).astype(BF16)
    b_hi = b.astype(BF16)
    b_lo = (b - b_hi.astype(F32)).astype(BF16)
    return (jnp.dot(a_hi, b_hi, preferred_element_type=F32) + jnp.dot(a_lo, b_hi, preferred_element_type=F32)
            + jnp.dot(a_hi, b_lo, preferred_element_type=F32))


def _sigmoid(x):
    return 0.5 * jnp.tanh(0.5 * x) + 0.5


def _log_sigmoid(x):
    return jnp.minimum(x, 0.0) - jnp.log1p(jnp.exp(-jnp.abs(x)))


def _layer_norm(y, g, b, eps):
    mu = jnp.mean(y, axis=-1, keepdims=True)
    d = y - mu
    var = jnp.mean(d * d, axis=-1, keepdims=True)
    return d * lax.rsqrt(var + eps) * g + b


def _ffn_ln_kernel(x_ref, wg_ref, wu_ref, wd_ref, g_ref, b_ref, o_ref, ob_ref, h_ref, *, alpha, tf):
    x = x_ref[...]
    xb = x.astype(BF16)
    d_ff = wg_ref.shape[1]
    for c in range(d_ff // tf):
        sl = slice(c * tf, (c + 1) * tf)
        gate = jnp.dot(xb, wg_ref[:, sl], preferred_element_type=F32)
        up = jnp.dot(xb, wu_ref[:, sl], preferred_element_type=F32)
        h_ref[:, sl] = (gate * _sigmoid(gate) * up).astype(BF16)
    y = alpha * x + 0.5 * jnp.dot(h_ref[...], wd_ref[...], preferred_element_type=F32)
    out = _layer_norm(y, g_ref[...], b_ref[...], LN_EPS)
    o_ref[...] = out
    ob_ref[...] = out.astype(BF16)


def _ffn_ln(x, wg, wu, wd, g, b, alpha, tm=512, tf=256):
    n, d = x.shape
    d_ff = wg.shape[1]
    const = lambda i: (0, 0)
    return pl.pallas_call(
        functools.partial(_ffn_ln_kernel, alpha=alpha, tf=tf),
        grid=(n // tm,),
        in_specs=[
            pl.BlockSpec((tm, d), lambda i: (i, 0)),
            pl.BlockSpec((d, d_ff), const, pipeline_mode=pl.Buffered(1)),
            pl.BlockSpec((d, d_ff), const, pipeline_mode=pl.Buffered(1)),
            pl.BlockSpec((d_ff, d), const, pipeline_mode=pl.Buffered(1)),
            pl.BlockSpec((1, d), const),
            pl.BlockSpec((1, d), const),
        ],
        out_specs=[pl.BlockSpec((tm, d), lambda i: (i, 0)), pl.BlockSpec((tm, d), lambda i: (i, 0))],
        out_shape=[jax.ShapeDtypeStruct((n, d), F32), jax.ShapeDtypeStruct((n, d), BF16)],
        scratch_shapes=[pltpu.VMEM((tm, d_ff), BF16)],
        compiler_params=pltpu.CompilerParams(dimension_semantics=("arbitrary",), vmem_limit_bytes=VMEM_LIMIT),
        name="ffn_ln",
    )(x, wg, wu, wd, g, b)


def _proj_kernel(x_ref, w_ref, om_ref, of_ref, og_ref):
    x = x_ref[...]
    n_m = om_ref.shape[1]
    n_f = of_ref.shape[1]
    om_ref[...] = jnp.dot(x, w_ref[:, :n_m], preferred_element_type=F32).astype(om_ref.dtype)
    of_ref[...] = jnp.dot(x, w_ref[:, n_m:n_m + n_f], preferred_element_type=F32)
    og_ref[...] = jnp.dot(x, w_ref[:, n_m + n_f:], preferred_element_type=F32).astype(og_ref.dtype)


def _proj(xb, w, n_m, n_f, tm=256):
    n, d = xb.shape
    n_g = w.shape[1] - n_m - n_f
    tile = lambda width: pl.BlockSpec((tm, width), lambda i: (i, 0))
    return pl.pallas_call(
        _proj_kernel,
        grid=(n // tm,),
        in_specs=[tile(d), pl.BlockSpec(w.shape, lambda i: (0, 0), pipeline_mode=pl.Buffered(1))],
        out_specs=[tile(n_m), tile(n_f), tile(n_g)],
        out_shape=[jax.ShapeDtypeStruct((n, n_m), BF16), jax.ShapeDtypeStruct((n, n_f), F32),
                   jax.ShapeDtypeStruct((n, n_g), BF16)],
        compiler_params=pltpu.CompilerParams(dimension_semantics=("arbitrary",), vmem_limit_bytes=VMEM_LIMIT),
        name="proj",
    )(xb, w)


def _mlstm_kernel(q_ref, k_ref, v_ref, o_ref, g_ref, cw_ref, cb_ref, gb_ref, ng_ref, out_ref,
                  c_ref, n_ref, m_ref, pq_ref, pk_ref, *, nh, dh):
    L = q_ref.shape[0]
    width = nh * dh
    halo = pq_ref.shape[0]
    step = pl.program_id(1)

    @pl.when(step == 0)
    def _():
        c_ref[...] = jnp.zeros_like(c_ref)
        n_ref[...] = jnp.zeros_like(n_ref)
        m_ref[...] = jnp.zeros_like(m_ref)
        pq_ref[...] = jnp.zeros_like(pq_ref)
        pk_ref[...] = jnp.zeros_like(pk_ref)

    sr = lax.broadcasted_iota(jnp.int32, ((M_CONV - 1) * L, halo + L), 0)
    sc = lax.broadcasted_iota(jnp.int32, ((M_CONV - 1) * L, halo + L), 1)
    shift = (sc == halo + sr % L - (sr // L + 1)).astype(BF16)

    def conv_silu(x_ref, p_ref, w_off):
        cur = x_ref[...]
        delayed = jnp.dot(shift, jnp.concatenate([p_ref[...], cur], axis=0), preferred_element_type=F32)
        p_ref[...] = cur[L - halo:, :]
        acc = cb_ref[:, w_off:w_off + width] + cur.astype(F32) * cw_ref[M_CONV - 1:M_CONV, w_off:w_off + width]
        for j in range(1, M_CONV):
            acc = acc + delayed[(j - 1) * L:j * L] * cw_ref[M_CONV - 1 - j:M_CONV - j, w_off:w_off + width]
        return acc * _sigmoid(acc)

    q_all = conv_silu(q_ref, pq_ref, 0) * (dh ** -0.5)
    k_all = conv_silu(k_ref, pk_ref, width)

    gates = g_ref[...] + gb_ref[...]
    log_f = _log_sigmoid(gates)
    ri = lax.broadcasted_iota(jnp.int32, (L, L), 0)
    ci = lax.broadcasted_iota(jnp.int32, (L, L), 1)
    causal = ri >= ci
    b_cols = _dot_exact_lhs(causal.astype(F32), log_f)
    gates_t = gates.T
    b_rows = b_cols.T

    heads = range(nh)
    hs = [slice(h * dh, (h + 1) * dh) for h in heads]
    q = [q_all[:, s] for s in hs]
    k = [k_all[:, s] for s in hs]
    v = [v_ref[:, s] for s in hs]
    i_col = [gates[:, h:h + 1] for h in heads]
    b_col = [b_cols[:, nh + h:nh + h + 1] for h in heads]
    i_row = [gates_t[h:h + 1, :] for h in heads]
    b_row = [b_rows[nh + h:nh + h + 1, :] for h in heads]
    g_tot = [b[L - 1:L, :] for b in b_col]
    c_prev = [c_ref[h] for h in heads]
    n_prev = [n_ref[8 * h:8 * h + 1, :] for h in heads]
    m_prev = [m_ref[8 * h:8 * h + 1, 0:1] for h in heads]

    a_col = [g_tot[h] - b_col[h] + i_col[h] for h in heads]
    m_loc = [jnp.max(z, axis=0, keepdims=True) for z in a_col]
    kw = [k[h] * jnp.exp(a_col[h] - m_loc[h]) for h in heads]
    c_loc = [_bdot_tn(kw[h], v[h]) for h in heads]
    n_loc = [jnp.sum(z, axis=0, keepdims=True) for z in kw]

    d_log = [jnp.where(causal, b_col[h] - b_row[h] + i_row[h], -jnp.inf) for h in heads]
    inter = [b_col[h] + m_prev[h] for h in heads]
    m_t = [jnp.maximum(jnp.max(d_log[h], axis=-1, keepdims=True), inter[h]) for h in heads]
    s = [_bdot_nt(q[h], k[h]) * jnp.exp(d_log[h] - m_t[h]) for h in heads]
    s_inter = [jnp.exp(inter[h] - m_t[h]) for h in heads]
    num = [_bdot(s[h], v[h]) + s_inter[h] * _bdot(q[h], c_prev[h]) for h in heads]
    den = [jnp.sum(s[h], axis=-1, keepdims=True) + s_inter[h] * jnp.sum(q[h] * n_prev[h], axis=-1, keepdims=True)
           for h in heads]
    hid = [num[h] / jnp.maximum(jnp.abs(den[h]), jnp.exp(-m_t[h])) for h in heads]

    m_new = [jnp.maximum(g_tot[h] + m_prev[h], m_loc[h]) for h in heads]
    for h in heads:
        s_old = jnp.exp(g_tot[h] + m_prev[h] - m_new[h])
        s_new = jnp.exp(m_loc[h] - m_new[h])
        c_ref[h] = s_old * c_prev[h] + s_new * c_loc[h]
        n_ref[8 * h:8 * h + 1, :] = s_old * n_prev[h] + s_new * n_loc[h]
        m_ref[8 * h:8 * h + 1, :] = jnp.broadcast_to(m_new[h], (1, m_ref.shape[1]))

    for h in heads:
        mu = jnp.mean(hid[h], axis=-1, keepdims=True)
        d = hid[h] - mu
        var = jnp.mean(d * d, axis=-1, keepdims=True)
        hn = d * lax.rsqrt(var + LN_EPS)
        out_ref[:, hs[h]] = (hn * ng_ref[:, hs[h]] * _sigmoid(o_ref[:, hs[h]].astype(F32))).astype(out_ref.dtype)


def _mlstm(pm, pg, gate_blk, conv_w, conv_b, gate_bias, norm_g, bsz, t):
    n = pm.shape[0]
    width = pm.shape[1] // 4
    nh = M_HEADS
    dh = width // nh
    L = M_CHUNK
    nc = t // L
    rows = lambda b, c: b * nc + c
    const = lambda b, c: (0, 0)
    return pl.pallas_call(
        functools.partial(_mlstm_kernel, nh=nh, dh=dh),
        grid=(bsz, nc),
        in_specs=[
            pl.BlockSpec((L, width), lambda b, c: (rows(b, c), 0)),
            pl.BlockSpec((L, width), lambda b, c: (rows(b, c), 1)),
            pl.BlockSpec((L, width), lambda b, c: (rows(b, c), 2)),
            pl.BlockSpec((L, width), lambda b, c: (rows(b, c), 3)),
            pl.BlockSpec((L, 128), lambda b, c: (rows(b, c), gate_blk)),
            pl.BlockSpec((M_CONV, 2 * width), const),
            pl.BlockSpec((1, 2 * width), const),
            pl.BlockSpec((1, 128), const),
            pl.BlockSpec((1, width), const),
        ],
        out_specs=pl.BlockSpec((L, width), lambda b, c: (rows(b, c), 0)),
        out_shape=jax.ShapeDtypeStruct((n, width), BF16),
        scratch_shapes=[
            pltpu.VMEM((nh, dh, dh), F32),
            pltpu.VMEM((8 * nh, dh), F32),
            pltpu.VMEM((8 * nh, 128), F32),
            pltpu.VMEM((M_HALO, width), BF16),
            pltpu.VMEM((M_HALO, width), BF16),
        ],
        compiler_params=pltpu.CompilerParams(dimension_semantics=("arbitrary", "arbitrary"),
                                             vmem_limit_bytes=VMEM_LIMIT),
        name="mlstm",
    )(pm, pm, pm, pm, pg, conv_w, conv_b, gate_bias, norm_g)


def _tri_solve_cat(nms, cmm, L, out):
    shape = nms[0].shape
    ri = lax.broadcasted_iota(jnp.int32, shape, 0)
    ci = lax.broadcasted_iota(jnp.int32, shape, 1) % L
    blk = lambda w: (ri // w) == (ci // w)
    eye = (ri == ci).astype(F32)
    diag = blk(16)
    nds = [jnp.where(diag, nm, 0.0) for nm in nms]
    ts = [eye + nd for nd in nds]
    pws = [cmm(nd, nd) for nd in nds]
    yield
    for _ in range(2):
        both = [cmm(jnp.concatenate([p, t], axis=0), p) for p, t in zip(pws, ts)]
        pws = [b[:L] for b in both]
        ts = [t + b[L:] for t, b in zip(ts, both)]
        yield
    ts = [t + cmm(t, p) for t, p in zip(ts, pws)]
    yield
    ys = [cmm(jnp.where(diag, 0.0, nm), t) for nm, t in zip(nms, ts)]
    yield
    both = [cmm(jnp.concatenate([y, t], axis=0), y) for y, t in zip(ys, ts)]
    y2s = [b[:L] for b in both]
    ts = [t + b[L:] for t, b in zip(ts, both)]
    yield
    out["t"] = [t + cmm(t, y2) for t, y2 in zip(ts, y2s)]


def _interleave(*tasks):
    tasks = list(tasks)
    while tasks:
        for task in list(tasks):
            try:
                next(task)
            except StopIteration:
                tasks.remove(task)


def _rwkv_kernel(xr_ref, xk_ref, xv_ref, xl_ref, mur_ref, muk_ref, muv_ref, mul_ref,
                 w0_ref, a0_ref, kkw_ref, kaw_ref, rrk_ref, gng_ref, gnb_ref,
                 w2_ref, a2_ref, g2_ref, out_ref,
                 s_ref, pr_ref, pk_ref, pv_ref, pl_ref, *, L, nchunk, hd, gw):
    R = xr_ref.shape[0]
    width = xr_ref.shape[1]
    ngroups = width // gw
    hpg = gw // hd
    step = pl.program_id(1)

    @pl.when(step == 0)
    def _():
        s_ref[...] = jnp.zeros_like(s_ref)
        pr_ref[...] = jnp.zeros_like(pr_ref)
        pk_ref[...] = jnp.zeros_like(pk_ref)
        pv_ref[...] = jnp.zeros_like(pv_ref)
        pl_ref[...] = jnp.zeros_like(pl_ref)

    row_r = lax.broadcasted_iota(jnp.int32, (R, 1), 0)
    row_l = lax.broadcasted_iota(jnp.int32, (L, 1), 0)
    carry = {}
    for key, x_ref, p_ref in (("r", xr_ref, pr_ref), ("k", xk_ref, pk_ref), ("v", xv_ref, pv_ref),
                              ("l", xl_ref, pl_ref)):
        carry[key] = p_ref[0:1, :]
        p_ref[0:1, :] = x_ref[R - 1:R, :]

    def shift_mix(x_ref, key, mu_ref, c, cs):
        x = x_ref[c * L:(c + 1) * L, cs]
        first = carry[key][:, cs] if c == 0 else x_ref[c * L - 1:c * L, cs]
        prev = jnp.where(row_l == 0, first, pltpu.roll(x, 1, 0))
        return x + (prev - x) * mu_ref[:, cs]

    li = lax.broadcasted_iota(jnp.int32, (gw, gw), 0)
    lj = lax.broadcasted_iota(jnp.int32, (gw, gw), 1)
    same_head = (li // hd) == (lj // hd)
    head_ones = same_head.astype(BF16)

    def seg_sums(zs):
        hi = [z.astype(BF16) for z in zs]
        lo = [(z - h.astype(F32)).astype(BF16) for z, h in zip(zs, hi)]
        sums = jnp.dot(jnp.concatenate(hi + lo, axis=0), head_ones, preferred_element_type=F32)
        n, rows = len(zs), zs[0].shape[0]
        return [sums[i * rows:(i + 1) * rows] + sums[(n + i) * rows:(n + i + 1) * rows] for i in range(n)]

    def stack(z):
        zb = z.astype(BF16)
        return jnp.where(same_head, jnp.concatenate([zb] * hpg, axis=0), jnp.zeros((), BF16))

    def cmm(a_cat, b):
        return jnp.dot(a_cat.astype(BF16), stack(b), preferred_element_type=F32)

    ti = lax.broadcasted_iota(jnp.int32, (L, L), 0)
    tj = lax.broadcasted_iota(jnp.int32, (L, L), 1)
    tri = (ti >= tj).astype(F32)
    ci_ = lax.broadcasted_iota(jnp.int32, (L, gw), 0)
    cj_ = lax.broadcasted_iota(jnp.int32, (L, gw), 1) % L
    strict = ci_ > cj_
    incl = ci_ >= cj_

    groups = range(ngroups)
    cols = [slice(grp * gw, (grp + 1) * gw) for grp in groups]

    xl = xl_ref[...]
    xl_prev = jnp.where(row_r == 0, carry["l"], pltpu.roll(xl, 1, 0))
    xl = xl + (xl_prev - xl) * mul_ref[...]
    xl_tanh = jnp.tanh(xl)
    xl_sig = _sigmoid(xl)
    wl_pre = [w0_ref[:, cs] + _dot_hp(xl_tanh, w2_ref[:, cs]) for cs in cols]
    a_pre = [a0_ref[:, cs] + _bdot(xl, a2_ref[:, cs]) for cs in cols]
    gate = [_bdot(xl_sig, g2_ref[:, cs]) for cs in cols]

    ctx = [dict() for _ in range(nchunk)]

    def setup_task(c):
        p = ctx[c]
        rs = slice(c * L, (c + 1) * L)
        rr = [shift_mix(xr_ref, "r", mur_ref, c, cs) for cs in cols]
        yield
        rk = [shift_mix(xk_ref, "k", muk_ref, c, cs) for cs in cols]
        yield
        rv = [shift_mix(xv_ref, "v", muv_ref, c, cs) for cs in cols]
        p["v"] = rv
        yield
        lw = [-jnp.exp(_log_sigmoid(z[rs]) - 0.5) for z in wl_pre]
        yield
        a = [_sigmoid(z[rs]) for z in a_pre]
        kk = [x * kkw_ref[:, cs] for x, cs in zip(rk, cols)]
        k2 = [x * (1.0 + (ai - 1.0) * kaw_ref[:, cs]) for x, ai, cs in zip(rk, a, cols)]
        yield
        sums = seg_sums([x * x for x in kk] + [r * k * rrk_ref[:, cs] for r, k, cs in zip(rr, k2, cols)])
        yield
        kk = [x / jnp.maximum(jnp.sqrt(s), 1e-12) for x, s in zip(kk, sums[:ngroups])]
        bv = [x * ai for x, ai in zip(kk, a)]
        p["bonus"] = [s * v for s, v in zip(sums[ngroups:], rv)]
        yield
        cum_c = [_dot_exact_lhs(tri, z) for z in lw]
        cum_l = [z[L - 1:L, :] for z in cum_c]
        p["w_l"] = [jnp.exp(z) for z in cum_l]
        yield
        e_neg = [jnp.exp(-z) for z in cum_c]
        yield
        bk_stack = [jnp.concatenate([stack(b * e), stack(k * e)], axis=0) for b, k, e in zip(bv, k2, e_neg)]
        yield
        at = [-x * jnp.exp(z - w) for x, z, w in zip(kk, cum_c, lw)]
        yield
        rt = [x * jnp.exp(z) for x, z in zip(rr, cum_c)]
        p["ar"] = [jnp.concatenate([x, y], axis=0) for x, y in zip(at, rt)]
        yield
        aa = [_bdot_nt(x, y) for x, y in zip(p["ar"], bk_stack)]
        yield
        e_end = [jnp.exp(zl - z) for zl, z in zip(cum_l, cum_c)]
        p["bk_end"] = [jnp.concatenate([b * e, k * e], axis=0) for b, k, e in zip(bv, k2, e_end)]
        yield
        p["nm"] = [jnp.where(strict, z[:L, :gw], 0.0) for z in aa]
        p["arb"] = [jnp.where(incl, z[L:, :gw], 0.0) for z in aa]
        yield
        ak = [jnp.concatenate([jnp.where(strict, z[:L, gw:], 0.0), jnp.where(incl, z[L:, gw:], 0.0)], axis=0)
              for z in aa]
        yield
        p["akv"] = [cmm(x, v) for x, v in zip(ak, p["v"])]

    def solve_task(c):
        yield from _tri_solve_cat(ctx[c]["nm"], cmm, L, ctx[c])

    def tail_task(c):
        p = ctx[c]
        rs = slice(c * L, (c + 1) * L)
        s0 = [s_ref[grp] for grp in groups]
        ps = [_bdot_nt(x, s) for x, s in zip(p["ar"], s0)]
        yield
        u = [cmm(t, z[:L] + w[:L]) for t, z, w in zip(p["t"], ps, p["akv"])]
        yield
        y = [z[L:] + cmm(x, ui) + w[L:] for z, x, ui, w in zip(ps, p["arb"], u, p["akv"])]
        yield
        upd = [_bdot_tn(jnp.concatenate([ui, v], axis=0), x) for ui, v, x in zip(u, p["v"], p["bk_end"])]
        yield
        for grp in groups:
            s_ref[grp] = s0[grp] * p["w_l"][grp] + jnp.where(same_head, upd[grp], 0.0)
        yield
        mu = [z * (1.0 / hd) for z in seg_sums(y)]
        yield
        d = [z - m for z, m in zip(y, mu)]
        var = [z * (1.0 / hd) for z in seg_sums([z * z for z in d])]
        yield
        for grp in groups:
            cs = cols[grp]
            yn = d[grp] * lax.rsqrt(var[grp] + R_GN_EPS) * gng_ref[:, cs] + gnb_ref[:, cs]
            out_ref[rs, cs] = ((yn + p["bonus"][grp]) * gate[grp][rs]).astype(out_ref.dtype)

    for k in range(nchunk + 2):
        tasks = []
        if 0 <= k - 2 < nchunk:
            tasks.append(tail_task(k - 2))
        if 0 <= k - 1 < nchunk:
            tasks.append(solve_task(k - 1))
        if k < nchunk:
            tasks.append(setup_task(k))
        _interleave(*tasks)


def _rwkv(pr, r_mu, w0, a0, kkw, kaw, rrk, gng, gnb, w2p, a2p, g2p, bsz, t):
    n = pr.shape[0]
    width = w0.shape[1]
    hd = R_HEAD_DIM
    gw = R_GROUP_HEADS * hd
    L = R_CHUNK
    R = L * R_CHUNKS_PER_STEP
    ns = t // R
    lr = w2p.shape[0]
    lblk = (3 * width) // lr
    rows = lambda b, c: b * ns + c
    xspec = lambda off: pl.BlockSpec((R, width), lambda b, c: (rows(b, c), off))
    pspec = lambda off: pl.BlockSpec((1, width), lambda b, c: (0, off))
    gspec = pl.BlockSpec((1, width), lambda b, c: (0, 0))
    wspec = pl.BlockSpec((lr, width), lambda b, c: (0, 0))
    return pl.pallas_call(
        functools.partial(_rwkv_kernel, L=L, nchunk=R_CHUNKS_PER_STEP, hd=hd, gw=gw),
        grid=(bsz, ns),
        in_specs=[
            xspec(0), xspec(1), xspec(2),
            pl.BlockSpec((R, lr), lambda b, c: (rows(b, c), lblk)),
            pspec(0), pspec(1), pspec(2),
            pl.BlockSpec((1, lr), lambda b, c: (0, lblk)),
            gspec, gspec, gspec, gspec, gspec, gspec, gspec,
            wspec, wspec, wspec,
        ],
        out_specs=pl.BlockSpec((R, width), lambda b, c: (rows(b, c), 0)),
        out_shape=jax.ShapeDtypeStruct((n, width), BF16),
        scratch_shapes=[
            pltpu.VMEM((width // gw, gw, gw), F32),
            pltpu.VMEM((8, width), F32),
            pltpu.VMEM((8, width), F32),
            pltpu.VMEM((8, width), F32),
            pltpu.VMEM((8, lr), F32),
        ],
        compiler_params=pltpu.CompilerParams(dimension_semantics=("arbitrary", "arbitrary"),
                                             vmem_limit_bytes=VMEM_LIMIT),
        name="rwkv",
    )(pr, pr, pr, pr, r_mu, r_mu, r_mu, r_mu, w0, a0, kkw, kaw, rrk, gng, gnb, w2p, a2p, g2p)


def _post_kernel(hm_ref, yr_ref, ga_ref, gb_ref, x_ref, wa_ref, wb_ref, wo_ref, g_ref, b_ref, o_ref, *, alpha):
    ya = jnp.dot(hm_ref[...], wa_ref[...], preferred_element_type=F32)
    yb = jnp.dot(yr_ref[...], wb_ref[...], preferred_element_type=F32)
    merged = _sigmoid(ga_ref[...].astype(F32)) * ya + _sigmoid(gb_ref[...].astype(F32)) * yb
    mix = jnp.dot(merged.astype(BF16), wo_ref[...], preferred_element_type=F32)
    o_ref[...] = _layer_norm(alpha * x_ref[...] + mix, g_ref[...], b_ref[...], LN_EPS)


def _post(hm, yr, pgate, x1, wa, wb, wo, g, b, alpha, tm=512):
    n, d = x1.shape
    tile = lambda j: pl.BlockSpec((tm, d), lambda i: (i, j))
    wspec = pl.BlockSpec((d, d), lambda i: (0, 0), pipeline_mode=pl.Buffered(1))
    vspec = pl.BlockSpec((1, d), lambda i: (0, 0))
    return pl.pallas_call(
        functools.partial(_post_kernel, alpha=alpha),
        grid=(n // tm,),
        in_specs=[tile(0), tile(0), tile(0), tile(1), tile(0), wspec, wspec, wspec, vspec, vspec],
        out_specs=tile(0),
        out_shape=jax.ShapeDtypeStruct((n, d), F32),
        compiler_params=pltpu.CompilerParams(dimension_semantics=("arbitrary",), vmem_limit_bytes=VMEM_LIMIT),
        name="post",
    )(hm, yr, pgate, pgate, x1, wa, wb, wo, g, b)


def kernel(x, ffn1_w_gate, ffn1_w_up, ffn1_w_down, ln1_g, ln1_b, w_in, m_conv_w, m_conv_b, m_i_bias, m_f_bias,
           m_norm_g, r_mu, r_w0, r_w2, r_a0, r_a2, r_g2, r_k_k, r_k_a, r_r_k, r_gn_g, r_gn_b, w_branch_a,
           w_branch_b, w_out, ln2_g, ln2_b, ffn2_w_gate, ffn2_w_up, ffn2_w_down, ln3_g, ln3_b):
    bsz, t, d = x.shape
    depth = w_in.shape[0]
    alpha = (2 * depth) ** 0.25
    mw = d
    rw = d
    nh = M_HEADS
    lr = R_DECAY_RANK + R_AAA_RANK + R_GATE_RANK
    row = lambda p: p.reshape(1, -1)

    cur = x.reshape(bsz * t, d)
    for l in range(depth):
        x1, x1b = _ffn_ln(cur, ffn1_w_gate[l].astype(BF16), ffn1_w_up[l].astype(BF16),
                          ffn1_w_down[l].astype(BF16), row(ln1_g[l]), row(ln1_b[l]), alpha)

        w = w_in[l]
        o_gate = 4 * mw
        o_rc = o_gate + 2 * nh
        o_ga = o_rc + 3 * rw + lr
        w_all = jnp.concatenate(
            [w[:, :o_gate], w[:, o_rc:o_ga], jnp.pad(w[:, o_gate:o_rc], ((0, 0), (0, 128 - 2 * nh))), w[:, o_ga:]],
            axis=1).astype(BF16)
        pm, pr, pgate = _proj(x1b, w_all, o_gate, o_ga - o_rc + 128)

        gate_bias = jnp.pad(jnp.concatenate([m_i_bias[l], m_f_bias[l]]), (0, 128 - 2 * nh)).reshape(1, 128)
        hm = _mlstm(pm, pr, (o_ga - o_rc) // 128, m_conv_w[l], row(m_conv_b[l]), gate_bias, row(m_norm_g[l]),
                    bsz, t)

        zeros = lambda r: jnp.zeros((r, rw), F32)
        w2p = jnp.concatenate([r_w2[l], zeros(R_AAA_RANK + R_GATE_RANK)], axis=0)
        a2p = jnp.concatenate([zeros(R_DECAY_RANK), r_a2[l], zeros(R_GATE_RANK)], axis=0)
        g2p = jnp.concatenate([zeros(R_DECAY_RANK + R_AAA_RANK), r_g2[l]], axis=0)
        yr = _rwkv(pr, row(r_mu[l]), row(r_w0[l]), row(r_a0[l]), row(r_k_k[l]), row(r_k_a[l]), row(r_r_k[l]),
                   row(r_gn_g[l]), row(r_gn_b[l]), w2p, a2p, g2p, bsz, t)

        x2 = _post(hm, yr, pgate, x1, w_branch_a[l].astype(BF16), w_branch_b[l].astype(BF16),
                   w_out[l].astype(BF16), row(ln2_g[l]), row(ln2_b[l]), alpha)
        cur, _ = _ffn_ln(x2, ffn2_w_gate[l].astype(BF16), ffn2_w_up[l].astype(BF16),
                         ffn2_w_down[l].astype(BF16), row(ln3_g[l]), row(ln3_b[l]), alpha)
    return cur.reshape(bsz, t, d)
```

```python
import functools

import jax
import jax.numpy as jnp
from jax import lax
from jax.experimental import pallas as pl
from jax.experimental.pallas import tpu as pltpu

F32 = jnp.float32
BF16 = jnp.bfloat16

M_HEADS = 4
M_CONV = 4
M_CHUNK = 128
M_HALO = 16
R_HEAD_DIM = 64
R_DECAY_RANK = 64
R_AAA_RANK = 64
R_GATE_RANK = 128
R_GN_EPS = 64e-5
LN_EPS = 1e-5

R_CHUNK = 64
R_GROUP_HEADS = 2
R_CHUNKS_PER_STEP = 8
VMEM_LIMIT = 56 * 1024 * 1024


def _bdot(a, b):
    return jnp.dot(a.astype(BF16), b.astype(BF16), preferred_element_type=F32)


def _bdot_nt(a, b):
    return lax.dot_general(a.astype(BF16), b.astype(BF16), (((1,), (1,)), ((), ())),
                           preferred_element_type=F32)


def _bdot_tn(a, b):
    return lax.dot_general(a.astype(BF16), b.astype(BF16), (((0,), (0,)), ((), ())),
                           preferred_element_type=F32)


def _split3(x):
    hi = x.astype(BF16)
    r1 = x - hi.astype(F32)
    mid = r1.astype(BF16)
    lo = (r1 - mid.astype(F32)).astype(BF16)
    return hi, mid, lo


def _dot_exact_lhs(m01, x):
    m = m01.astype(BF16)
    hi, mid, lo = _split3(x)
    return (jnp.dot(m, hi, preferred_element_type=F32) + jnp.dot(m, mid, preferred_element_type=F32)
            + jnp.dot(m, lo, preferred_element_type=F32))


def _dot_exact_rhs(x, m01):
    m = m01.astype(BF16)
    hi, mid, lo = _split3(x)
    return (jnp.dot(hi, m, preferred_element_type=F32) + jnp.dot(mid, m, preferred_element_type=F32)
            + jnp.dot(lo, m, preferred_element_type=F32))


def _dot_hp(a, b):
    a_hi = a.astype(BF16)
    a_lo = (a - a_hi.astype(F32)).astype(BF16)
    b_hi = b.astype(BF16)
    b_lo = (b - b_hi.astype(F32)).astype(BF16)
    return (jnp.dot(a_hi, b_hi, preferred_element_type=F32) + jnp.dot(a_lo, b_hi, preferred_element_type=F32)
            + jnp.dot(a_hi, b_lo, preferred_element_type=F32))


def _sigmoid(x):
    return 0.5 * jnp.tanh(0.5 * x) + 0.5


def _log_sigmoid(x):
    return jnp.minimum(x, 0.0) - jnp.log1p(jnp.exp(-jnp.abs(x)))


def _layer_norm(y, g, b, eps):
    mu = jnp.mean(y, axis=-1, keepdims=True)
    d = y - mu
    var = jnp.mean(d * d, axis=-1, keepdims=True)
    return d * lax.rsqrt(var + eps) * g + b


def _ffn_ln_kernel(x_ref, wg_ref, wu_ref, wd_ref, g_ref, b_ref, o_ref, ob_ref, h_ref, *, alpha, tf):
    x = x_ref[...]
    xb = x.astype(BF16)
    d_ff = wg_ref.shape[1]
    for c in range(d_ff // tf):
        sl = slice(c * tf, (c + 1) * tf)
        gate = jnp.dot(xb, wg_ref[:, sl], preferred_element_type=F32)
        up = jnp.dot(xb, wu_ref[:, sl], preferred_element_type=F32)
        h_ref[:, sl] = (gate * _sigmoid(gate) * up).astype(BF16)
    y = alpha * x + 0.5 * jnp.dot(h_ref[...], wd_ref[...], preferred_element_type=F32)
    out = _layer_norm(y, g_ref[...], b_ref[...], LN_EPS)
    o_ref[...] = out
    ob_ref[...] = out.astype(BF16)


def _ffn_ln(x, wg, wu, wd, g, b, alpha, tm=1024, tf=256):
    n, d = x.shape
    d_ff = wg.shape[1]
    const = lambda i: (0, 0)
    return pl.pallas_call(
        functools.partial(_ffn_ln_kernel, alpha=alpha, tf=tf),
        grid=(n // tm,),
        in_specs=[
            pl.BlockSpec((tm, d), lambda i: (i, 0)),
            pl.BlockSpec((d, d_ff), const, pipeline_mode=pl.Buffered(1)),
            pl.BlockSpec((d, d_ff), const, pipeline_mode=pl.Buffered(1)),
            pl.BlockSpec((d_ff, d), const, pipeline_mode=pl.Buffered(1)),
            pl.BlockSpec((1, d), const),
            pl.BlockSpec((1, d), const),
        ],
        out_specs=[pl.BlockSpec((tm, d), lambda i: (i, 0)), pl.BlockSpec((tm, d), lambda i: (i, 0))],
        out_shape=[jax.ShapeDtypeStruct((n, d), F32), jax.ShapeDtypeStruct((n, d), BF16)],
        scratch_shapes=[pltpu.VMEM((tm, d_ff), BF16)],
        compiler_params=pltpu.CompilerParams(dimension_semantics=("arbitrary",), vmem_limit_bytes=VMEM_LIMIT),
        name="ffn_ln",
    )(x, wg, wu, wd, g, b)


def _proj_kernel(x_ref, w_ref, om_ref, of_ref, og_ref):
    x = x_ref[...]
    n_m = om_ref.shape[1]
    n_f = of_ref.shape[1]
    om_ref[...] = jnp.dot(x, w_ref[:, :n_m], preferred_element_type=F32).astype(om_ref.dtype)
    of_ref[...] = jnp.dot(x, w_ref[:, n_m:n_m + n_f], preferred_element_type=F32)
    og_ref[...] = jnp.dot(x, w_ref[:, n_m + n_f:], preferred_element_type=F32).astype(og_ref.dtype)


def _proj(xb, w, n_m, n_f, tm=256):
    n, d = xb.shape
    n_g = w.shape[1] - n_m - n_f
    tile = lambda width: pl.BlockSpec((tm, width), lambda i: (i, 0))
    return pl.pallas_call(
        _proj_kernel,
        grid=(n // tm,),
        in_specs=[tile(d), pl.BlockSpec(w.shape, lambda i: (0, 0), pipeline_mode=pl.Buffered(1))],
        out_specs=[tile(n_m), tile(n_f), tile(n_g)],
        out_shape=[jax.ShapeDtypeStruct((n, n_m), BF16), jax.ShapeDtypeStruct((n, n_f), F32),
                   jax.ShapeDtypeStruct((n, n_g), BF16)],
        compiler_params=pltpu.CompilerParams(dimension_semantics=("arbitrary",), vmem_limit_bytes=VMEM_LIMIT),
        name="proj",
    )(xb, w)


def _mlstm_kernel(q_ref, k_ref, v_ref, o_ref, g_ref, cw_ref, cb_ref, gb_ref, ng_ref, out_ref,
                  c_ref, n_ref, m_ref, pq_ref, pk_ref, *, nh, dh):
    L = q_ref.shape[0]
    width = nh * dh
    halo = pq_ref.shape[0]
    step = pl.program_id(1)

    @pl.when(step == 0)
    def _():
        c_ref[...] = jnp.zeros_like(c_ref)
        n_ref[...] = jnp.zeros_like(n_ref)
        m_ref[...] = jnp.zeros_like(m_ref)
        pq_ref[...] = jnp.zeros_like(pq_ref)
        pk_ref[...] = jnp.zeros_like(pk_ref)

    sr = lax.broadcasted_iota(jnp.int32, ((M_CONV - 1) * L, halo + L), 0)
    sc = lax.broadcasted_iota(jnp.int32, ((M_CONV - 1) * L, halo + L), 1)
    shift = (sc == halo + sr % L - (sr // L + 1)).astype(BF16)

    half_w = 0.5 * cw_ref[...]
    half_b = 0.5 * cb_ref[...]

    def conv_silu(x_ref, p_ref, w_off):
        cur = x_ref[...]
        delayed = jnp.dot(shift, jnp.concatenate([p_ref[...], cur], axis=0), preferred_element_type=F32)
        p_ref[...] = cur[L - halo:, :]
        acc = half_b[:, w_off:w_off + width] + cur.astype(F32) * half_w[M_CONV - 1:M_CONV, w_off:w_off + width]
        for j in range(1, M_CONV):
            acc = acc + delayed[(j - 1) * L:j * L] * half_w[M_CONV - 1 - j:M_CONV - j, w_off:w_off + width]
        return acc + acc * jnp.tanh(acc)

    q_all = conv_silu(q_ref, pq_ref, 0) * (dh ** -0.5)
    k_all = conv_silu(k_ref, pk_ref, width)

    gates = g_ref[...] + gb_ref[...]
    log_f = _log_sigmoid(gates)
    ri = lax.broadcasted_iota(jnp.int32, (L, L), 0)
    ci = lax.broadcasted_iota(jnp.int32, (L, L), 1)
    causal = ri >= ci
    b_cols = _dot_exact_lhs(causal.astype(F32), log_f)
    gates_t = gates.T
    b_rows = b_cols.T

    heads = range(nh)
    hs = [slice(h * dh, (h + 1) * dh) for h in heads]
    q = [q_all[:, s] for s in hs]
    k = [k_all[:, s] for s in hs]
    v = [v_ref[:, s] for s in hs]
    i_col = [gates[:, h:h + 1] for h in heads]
    b_col = [b_cols[:, nh + h:nh + h + 1] for h in heads]
    i_row = [gates_t[h:h + 1, :] for h in heads]
    b_row = [b_rows[nh + h:nh + h + 1, :] for h in heads]
    g_tot = [b[L - 1:L, :] for b in b_col]
    c_prev = [c_ref[h] for h in heads]
    n_prev = [n_ref[8 * h:8 * h + 1, :] for h in heads]
    m_prev = [m_ref[8 * h:8 * h + 1, 0:1] for h in heads]

    a_col = [g_tot[h] - b_col[h] + i_col[h] for h in heads]
    m_loc = [jnp.max(z, axis=0, keepdims=True) for z in a_col]
    m_new = [jnp.maximum(g_tot[h] + m_prev[h], m_loc[h]) for h in heads]
    kw = [k[h] * jnp.exp(a_col[h] - m_new[h]) for h in heads]
    c_loc = [_bdot_tn(kw[h], v[h]) for h in heads]
    n_loc = [jnp.sum(z, axis=0, keepdims=True) for z in kw]

    d_log = [jnp.where(causal, b_col[h] - b_row[h] + i_row[h], -jnp.inf) for h in heads]
    inter = [b_col[h] + m_prev[h] for h in heads]
    m_t = [jnp.maximum(jnp.max(d_log[h], axis=-1, keepdims=True), inter[h]) for h in heads]
    s = [_bdot_nt(q[h], k[h]) * jnp.exp(d_log[h] - m_t[h]) for h in heads]
    s_inter = [jnp.exp(inter[h] - m_t[h]) for h in heads]
    num = [_bdot(s[h], v[h]) + s_inter[h] * _bdot(q[h], c_prev[h]) for h in heads]
    den = [jnp.sum(s[h], axis=-1, keepdims=True) + s_inter[h] * jnp.sum(q[h] * n_prev[h], axis=-1, keepdims=True)
           for h in heads]
    hid = [num[h] / jnp.maximum(jnp.abs(den[h]), jnp.exp(-m_t[h])) for h in heads]

    for h in heads:
        s_old = jnp.exp(g_tot[h] + m_prev[h] - m_new[h])
        c_ref[h] = s_old * c_prev[h] + c_loc[h]
        n_ref[8 * h:8 * h + 1, :] = s_old * n_prev[h] + n_loc[h]
        m_ref[8 * h:8 * h + 1, :] = jnp.broadcast_to(m_new[h], (1, m_ref.shape[1]))

    for h in heads:
        mu = jnp.mean(hid[h], axis=-1, keepdims=True)
        d = hid[h] - mu
        var = jnp.mean(d * d, axis=-1, keepdims=True)
        hn = d * lax.rsqrt(var + LN_EPS)
        out_ref[:, hs[h]] = (hn * ng_ref[:, hs[h]] * _sigmoid(o_ref[:, hs[h]].astype(F32))).astype(out_ref.dtype)


def _mlstm(pm, pg, gate_blk, conv_w, conv_b, gate_bias, norm_g, bsz, t):
    n = pm.shape[0]
    width = pm.shape[1] // 4
    nh = M_HEADS
    dh = width // nh
    L = M_CHUNK
    nc = t // L
    rows = lambda b, c: b * nc + c
    const = lambda b, c: (0, 0)
    return pl.pallas_call(
        functools.partial(_mlstm_kernel, nh=nh, dh=dh),
        grid=(bsz, nc),
        in_specs=[
            pl.BlockSpec((L, width), lambda b, c: (rows(b, c), 0)),
            pl.BlockSpec((L, width), lambda b, c: (rows(b, c), 1)),
            pl.BlockSpec((L, width), lambda b, c: (rows(b, c), 2)),
            pl.BlockSpec((L, width), lambda b, c: (rows(b, c), 3)),
            pl.BlockSpec((L, 128), lambda b, c: (rows(b, c), gate_blk)),
            pl.BlockSpec((M_CONV, 2 * width), const),
            pl.BlockSpec((1, 2 * width), const),
            pl.BlockSpec((1, 128), const),
            pl.BlockSpec((1, width), const),
        ],
        out_specs=pl.BlockSpec((L, width), lambda b, c: (rows(b, c), 0)),
        out_shape=jax.ShapeDtypeStruct((n, width), BF16),
        scratch_shapes=[
            pltpu.VMEM((nh, dh, dh), F32),
            pltpu.VMEM((8 * nh, dh), F32),
            pltpu.VMEM((8 * nh, 128), F32),
            pltpu.VMEM((M_HALO, width), BF16),
            pltpu.VMEM((M_HALO, width), BF16),
        ],
        compiler_params=pltpu.CompilerParams(dimension_semantics=("arbitrary", "arbitrary"),
                                             vmem_limit_bytes=VMEM_LIMIT),
        name="mlstm",
    )(pm, pm, pm, pm, pg, conv_w, conv_b, gate_bias, norm_g)


def _tri_solve_cat(nms, cmm, L, out):
    shape = nms[0].shape
    ri = lax.broadcasted_iota(jnp.int32, shape, 0)
    ci = lax.broadcasted_iota(jnp.int32, shape, 1) % L
    blk = lambda w: (ri // w) == (ci // w)
    eye = (ri == ci).astype(F32)
    diag = blk(16)
    nds = [jnp.where(diag, nm, 0.0) for nm in nms]
    ts = [eye + nd for nd in nds]
    pws = [cmm(nd, nd) for nd in nds]
    yield
    for _ in range(2):
        both = [cmm(jnp.concatenate([p, t], axis=0), p) for p, t in zip(pws, ts)]
        pws = [b[:L] for b in both]
        ts = [t + b[L:] for t, b in zip(ts, both)]
        yield
    ts = [t + cmm(t, p) for t, p in zip(ts, pws)]
    yield
    ys = [cmm(jnp.where(diag, 0.0, nm), t) for nm, t in zip(nms, ts)]
    yield
    both = [cmm(jnp.concatenate([y, t], axis=0), y) for y, t in zip(ys, ts)]
    y2s = [b[:L] for b in both]
    ts = [t + b[L:] for t, b in zip(ts, both)]
    yield
    out["t"] = [t + cmm(t, y2) for t, y2 in zip(ts, y2s)]


def _interleave(*tasks):
    tasks = list(tasks)
    while tasks:
        for task in list(tasks):
            try:
                next(task)
            except StopIteration:
                tasks.remove(task)


def _rwkv_kernel(xr_ref, xk_ref, xv_ref, xl_ref, mur_ref, muk_ref, muv_ref, mul_ref,
                 w0_ref, a0_ref, kkw_ref, kaw_ref, rrk_ref, gng_ref, gnb_ref,
                 w2_ref, a2_ref, g2_ref, out_ref,
                 s_ref, pr_ref, pk_ref, pv_ref, pl_ref, *, L, nchunk, hd, gw):
    R = xr_ref.shape[0]
    width = xr_ref.shape[1]
    ngroups = width // gw
    hpg = gw // hd
    step = pl.program_id(1)

    @pl.when(step == 0)
    def _():
        s_ref[...] = jnp.zeros_like(s_ref)
        pr_ref[...] = jnp.zeros_like(pr_ref)
        pk_ref[...] = jnp.zeros_like(pk_ref)
        pv_ref[...] = jnp.zeros_like(pv_ref)
        pl_ref[...] = jnp.zeros_like(pl_ref)

    row_r = lax.broadcasted_iota(jnp.int32, (R, 1), 0)
    row_l = lax.broadcasted_iota(jnp.int32, (L, 1), 0)
    carry = {}
    for key, x_ref, p_ref in (("r", xr_ref, pr_ref), ("k", xk_ref, pk_ref), ("v", xv_ref, pv_ref),
                              ("l", xl_ref, pl_ref)):
        carry[key] = p_ref[0:1, :]
        p_ref[0:1, :] = x_ref[R - 1:R, :]

    def shift_mix(x_ref, key, mu_ref, c, cs):
        x = x_ref[c * L:(c + 1) * L, cs]
        first = carry[key][:, cs] if c == 0 else x_ref[c * L - 1:c * L, cs]
        prev = jnp.where(row_l == 0, first, pltpu.roll(x, 1, 0))
        return x + (prev - x) * mu_ref[:, cs]

    li = lax.broadcasted_iota(jnp.int32, (gw, gw), 0)
    lj = lax.broadcasted_iota(jnp.int32, (gw, gw), 1)
    same_head = (li // hd) == (lj // hd)
    lane_head = lax.broadcasted_iota(jnp.int32, (1, gw), 1) // hd

    def seg_sums(zs):
        outs = []
        for z in zs:
            acc = jnp.zeros_like(z)
            for h in range(hpg):
                mine = lane_head == h
                acc = jnp.where(mine, jnp.sum(jnp.where(mine, z, 0.0), axis=-1, keepdims=True), acc)
            outs.append(acc)
        return outs

    def stack(z):
        zb = z.astype(BF16)
        return jnp.where(same_head, jnp.concatenate([zb] * hpg, axis=0), jnp.zeros((), BF16))

    def cmm(a_cat, b):
        return jnp.dot(a_cat.astype(BF16), stack(b), preferred_element_type=F32)

    ti = lax.broadcasted_iota(jnp.int32, (L, L), 0)
    tj = lax.broadcasted_iota(jnp.int32, (L, L), 1)
    tri = (ti >= tj).astype(F32)
    ci_ = lax.broadcasted_iota(jnp.int32, (L, gw), 0)
    cj_ = lax.broadcasted_iota(jnp.int32, (L, gw), 1) % L
    strict = ci_ > cj_
    incl = ci_ >= cj_

    groups = range(ngroups)
    cols = [slice(grp * gw, (grp + 1) * gw) for grp in groups]

    xl = xl_ref[...]
    xl_prev = jnp.where(row_r == 0, carry["l"], pltpu.roll(xl, 1, 0))
    xl = xl + (xl_prev - xl) * mul_ref[...]
    xl_tanh = jnp.tanh(xl)
    xl_sig = _sigmoid(xl)
    wl_pre = [w0_ref[:, cs] + _dot_hp(xl_tanh, w2_ref[:, cs]) for cs in cols]
    a_pre = [a0_ref[:, cs] + _bdot(xl, a2_ref[:, cs]) for cs in cols]
    gate = [_bdot(xl_sig, g2_ref[:, cs]) for cs in cols]

    ctx = [dict() for _ in range(nchunk)]

    def setup_task(c):
        p = ctx[c]
        rs = slice(c * L, (c + 1) * L)
        rr = [shift_mix(xr_ref, "r", mur_ref, c, cs) for cs in cols]
        yield
        rk = [shift_mix(xk_ref, "k", muk_ref, c, cs) for cs in cols]
        yield
        rv = [shift_mix(xv_ref, "v", muv_ref, c, cs) for cs in cols]
        p["v"] = rv
        yield
        lw = [-jnp.exp(_log_sigmoid(z[rs]) - 0.5) for z in wl_pre]
        yield
        a = [_sigmoid(z[rs]) for z in a_pre]
        kk = [x * kkw_ref[:, cs] for x, cs in zip(rk, cols)]
        k2 = [x * (1.0 + (ai - 1.0) * kaw_ref[:, cs]) for x, ai, cs in zip(rk, a, cols)]
        yield
        sums = seg_sums([x * x for x in kk] + [r * k * rrk_ref[:, cs] for r, k, cs in zip(rr, k2, cols)])
        yield
        kk = [x / jnp.maximum(jnp.sqrt(s), 1e-12) for x, s in zip(kk, sums[:ngroups])]
        bv = [x * ai for x, ai in zip(kk, a)]
        p["bonus"] = [s * v for s, v in zip(sums[ngroups:], rv)]
        yield
        cum_c = [_dot_exact_lhs(tri, z) for z in lw]
        cum_l = [z[L - 1:L, :] for z in cum_c]
        p["w_l"] = [jnp.exp(z) for z in cum_l]
        yield
        e_neg = [jnp.exp(-z) for z in cum_c]
        yield
        bk_stack = [jnp.concatenate([stack(b * e), stack(k * e)], axis=0) for b, k, e in zip(bv, k2, e_neg)]
        yield
        at = [-x * jnp.exp(z - w) for x, z, w in zip(kk, cum_c, lw)]
        yield
        rt = [x * jnp.exp(z) for x, z in zip(rr, cum_c)]
        p["ar"] = [jnp.concatenate([x, y], axis=0) for x, y in zip(at, rt)]
        yield
        aa = [_bdot_nt(x, y) for x, y in zip(p["ar"], bk_stack)]
        yield
        e_end = [jnp.exp(zl - z) for zl, z in zip(cum_l, cum_c)]
        p["bk_end"] = [jnp.concatenate([b * e, k * e], axis=0) for b, k, e in zip(bv, k2, e_end)]
        yield
        p["nm"] = [jnp.where(strict, z[:L, :gw], 0.0) for z in aa]
        p["arb"] = [jnp.where(incl, z[L:, :gw], 0.0) for z in aa]
        yield
        ak = [jnp.concatenate([jnp.where(strict, z[:L, gw:], 0.0), jnp.where(incl, z[L:, gw:], 0.0)], axis=0)
              for z in aa]
        yield
        p["akv"] = [cmm(x, v) for x, v in zip(ak, p["v"])]

    def solve_task(c):
        yield from _tri_solve_cat(ctx[c]["nm"], cmm, L, ctx[c])

    def tail_task(c):
        p = ctx[c]
        rs = slice(c * L, (c + 1) * L)
        s0 = [s_ref[grp] for grp in groups]
        ps = [_bdot_nt(x, s) for x, s in zip(p["ar"], s0)]
        yield
        u = [cmm(t, z[:L] + w[:L]) for t, z, w in zip(p["t"], ps, p["akv"])]
        yield
        y = [z[L:] + cmm(x, ui) + w[L:] for z, x, ui, w in zip(ps, p["arb"], u, p["akv"])]
        yield
        upd = [_bdot_tn(jnp.concatenate([ui, v], axis=0), x) for ui, v, x in zip(u, p["v"], p["bk_end"])]
        yield
        for grp in groups:
            s_ref[grp] = s0[grp] * p["w_l"][grp] + jnp.where(same_head, upd[grp], 0.0)
        yield
        mu = [z * (1.0 / hd) for z in seg_sums(y)]
        yield
        d = [z - m for z, m in zip(y, mu)]
        var = [z * (1.0 / hd) for z in seg_sums([z * z for z in d])]
        yield
        for grp in groups:
            cs = cols[grp]
            yn = d[grp] * lax.rsqrt(var[grp] + R_GN_EPS) * gng_ref[:, cs] + gnb_ref[:, cs]
            out_ref[rs, cs] = ((yn + p["bonus"][grp]) * gate[grp][rs]).astype(out_ref.dtype)

    for k in range(nchunk + 2):
        tasks = []
        if 0 <= k - 2 < nchunk:
            tasks.append(tail_task(k - 2))
        if 0 <= k - 1 < nchunk:
            tasks.append(solve_task(k - 1))
        if k < nchunk:
            tasks.append(setup_task(k))
        _interleave(*tasks)


def _rwkv(pr, r_mu, w0, a0, kkw, kaw, rrk, gng, gnb, w2p, a2p, g2p, bsz, t):
    n = pr.shape[0]
    width = w0.shape[1]
    hd = R_HEAD_DIM
    gw = R_GROUP_HEADS * hd
    L = R_CHUNK
    R = L * R_CHUNKS_PER_STEP
    ns = t // R
    lr = w2p.shape[0]
    lblk = (3 * width) // lr
    rows = lambda b, c: b * ns + c
    xspec = lambda off: pl.BlockSpec((R, width), lambda b, c: (rows(b, c), off))
    pspec = lambda off: pl.BlockSpec((1, width), lambda b, c: (0, off))
    gspec = pl.BlockSpec((1, width), lambda b, c: (0, 0))
    wspec = pl.BlockSpec((lr, width), lambda b, c: (0, 0))
    return pl.pallas_call(
        functools.partial(_rwkv_kernel, L=L, nchunk=R_CHUNKS_PER_STEP, hd=hd, gw=gw),
        grid=(bsz, ns),
        in_specs=[
            xspec(0), xspec(1), xspec(2),
            pl.BlockSpec((R, lr), lambda b, c: (rows(b, c), lblk)),
            pspec(0), pspec(1), pspec(2),
            pl.BlockSpec((1, lr), lambda b, c: (0, lblk)),
            gspec, gspec, gspec, gspec, gspec, gspec, gspec,
            wspec, wspec, wspec,
        ],
        out_specs=pl.BlockSpec((R, width), lambda b, c: (rows(b, c), 0)),
        out_shape=jax.ShapeDtypeStruct((n, width), BF16),
        scratch_shapes=[
            pltpu.VMEM((width // gw, gw, gw), F32),
            pltpu.VMEM((8, width), F32),
            pltpu.VMEM((8, width), F32),
            pltpu.VMEM((8, width), F32),
            pltpu.VMEM((8, lr), F32),
        ],
        compiler_params=pltpu.CompilerParams(dimension_semantics=("arbitrary", "arbitrary"),
                                             vmem_limit_bytes=VMEM_LIMIT),
        name="rwkv",
    )(pr, pr, pr, pr, r_mu, r_mu, r_mu, r_mu, w0, a0, kkw, kaw, rrk, gng, gnb, w2p, a2p, g2p)


def _post_kernel(hm_ref, yr_ref, ga_ref, gb_ref, x_ref, wa_ref, wb_ref, wo_ref, g_ref, b_ref, o_ref, *, alpha):
    ya = jnp.dot(hm_ref[...], wa_ref[...], preferred_element_type=F32)
    yb = jnp.dot(yr_ref[...], wb_ref[...], preferred_element_type=F32)
    merged = _sigmoid(ga_ref[...].astype(F32)) * ya + _sigmoid(gb_ref[...].astype(F32)) * yb
    mix = jnp.dot(merged.astype(BF16), wo_ref[...], preferred_element_type=F32)
    o_ref[...] = _layer_norm(alpha * x_ref[...] + mix, g_ref[...], b_ref[...], LN_EPS)


def _post(hm, yr, pgate, x1, wa, wb, wo, g, b, alpha, tm=512):
    n, d = x1.shape
    tile = lambda j: pl.BlockSpec((tm, d), lambda i: (i, j))
    wspec = pl.BlockSpec((d, d), lambda i: (0, 0), pipeline_mode=pl.Buffered(1))
    vspec = pl.BlockSpec((1, d), lambda i: (0, 0))
    return pl.pallas_call(
        functools.partial(_post_kernel, alpha=alpha),
        grid=(n // tm,),
        in_specs=[tile(0), tile(0), tile(0), tile(1), tile(0), wspec, wspec, wspec, vspec, vspec],
        out_specs=tile(0),
        out_shape=jax.ShapeDtypeStruct((n, d), F32),
        compiler_params=pltpu.CompilerParams(dimension_semantics=("arbitrary",), vmem_limit_bytes=VMEM_LIMIT),
        name="post",
    )(hm, yr, pgate, pgate, x1, wa, wb, wo, g, b)


def kernel(x, ffn1_w_gate, ffn1_w_up, ffn1_w_down, ln1_g, ln1_b, w_in, m_conv_w, m_conv_b, m_i_bias, m_f_bias,
           m_norm_g, r_mu, r_w0, r_w2, r_a0, r_a2, r_g2, r_k_k, r_k_a, r_r_k, r_gn_g, r_gn_b, w_branch_a,
           w_branch_b, w_out, ln2_g, ln2_b, ffn2_w_gate, ffn2_w_up, ffn2_w_down, ln3_g, ln3_b):
    bsz, t, d = x.shape
    depth = w_in.shape[0]
    alpha = (2 * depth) ** 0.25
    mw = d
    rw = d
    nh = M_HEADS
    lr = R_DECAY_RANK + R_AAA_RANK + R_GATE_RANK
    row = lambda p: p.reshape(1, -1)

    cur = x.reshape(bsz * t, d)
    for l in range(depth):
        x1, x1b = _ffn_ln(cur, ffn1_w_gate[l].astype(BF16), ffn1_w_up[l].astype(BF16),
                          ffn1_w_down[l].astype(BF16), row(ln1_g[l]), row(ln1_b[l]), alpha)

        w = w_in[l]
        o_gate = 4 * mw
        o_rc = o_gate + 2 * nh
        o_ga = o_rc + 3 * rw + lr
        w_all = jnp.concatenate(
            [w[:, :o_gate], w[:, o_rc:o_ga], jnp.pad(w[:, o_gate:o_rc], ((0, 0), (0, 128 - 2 * nh))), w[:, o_ga:]],
            axis=1).astype(BF16)
        pm, pr, pgate = _proj(x1b, w_all, o_gate, o_ga - o_rc + 128)

        gate_bias = jnp.pad(jnp.concatenate([m_i_bias[l], m_f_bias[l]]), (0, 128 - 2 * nh)).reshape(1, 128)
        hm = _mlstm(pm, pr, (o_ga - o_rc) // 128, m_conv_w[l], row(m_conv_b[l]), gate_bias, row(m_norm_g[l]),
                    bsz, t)

        zeros = lambda r: jnp.zeros((r, rw), F32)
        w2p = jnp.concatenate([r_w2[l], zeros(R_AAA_RANK + R_GATE_RANK)], axis=0)
        a2p = jnp.concatenate([zeros(R_DECAY_RANK), r_a2[l], zeros(R_GATE_RANK)], axis=0)
        g2p = jnp.concatenate([zeros(R_DECAY_RANK + R_AAA_RANK), r_g2[l]], axis=0)
        yr = _rwkv(pr, row(r_mu[l]), row(r_w0[l]), row(r_a0[l]), row(r_k_k[l]), row(r_k_a[l]), row(r_r_k[l]),
                   row(r_gn_g[l]), row(r_gn_b[l]), w2p, a2p, g2p, bsz, t)

        x2 = _post(hm, yr, pgate, x1, w_branch_a[l].astype(BF16), w_branch_b[l].astype(BF16),
                   w_out[l].astype(BF16), row(ln2_g[l]), row(ln2_b[l]), alpha)
        cur, _ = _ffn_ln(x2, ffn2_w_gate[l].astype(BF16), ffn2_w_up[l].astype(BF16),
                         ffn2_w_down[l].astype(BF16), row(ln3_g[l]), row(ln3_b[l]), alpha)
    return cur.reshape(bsz, t, d)
```

```python
import functools

import jax
import jax.numpy as jnp
from jax import lax
from jax.experimental import pallas as pl
from jax.experimental.pallas import tpu as pltpu

F32 = jnp.float32
BF16 = jnp.bfloat16

M_HEADS = 4
M_CONV = 4
M_CHUNK = 128
M_HALO = 16
R_HEAD_DIM = 64
R_DECAY_RANK = 64
R_AAA_RANK = 64
R_GATE_RANK = 128
R_GN_EPS = 64e-5
LN_EPS = 1e-5

R_CHUNK = 64
R_GROUP_HEADS = 2
R_CHUNKS_PER_STEP = 8
P_TILE = 256
P_COLS = 256
VMEM_LIMIT = 56 * 1024 * 1024


def _bdot(a, b):
    return jnp.dot(a.astype(BF16), b.astype(BF16), preferred_element_type=F32)


def _bdot_nt(a, b):
    return lax.dot_general(a.astype(BF16), b.astype(BF16), (((1,), (1,)), ((), ())),
                           preferred_element_type=F32)


def _bdot_tn(a, b):
    return lax.dot_general(a.astype(BF16), b.astype(BF16), (((0,), (0,)), ((), ())),
                           preferred_element_type=F32)


def _split3(x):
    hi = x.astype(BF16)
    r1 = x - hi.astype(F32)
    mid = r1.astype(BF16)
    lo = (r1 - mid.astype(F32)).astype(BF16)
    return hi, mid, lo


def _dot_exact_lhs(m01, x):
    m = m01.astype(BF16)
    hi, mid, lo = _split3(x)
    return (jnp.dot(m, hi, preferred_element_type=F32) + jnp.dot(m, mid, preferred_element_type=F32)
            + jnp.dot(m, lo, preferred_element_type=F32))


def _dot_exact_rhs(x, m01):
    m = m01.astype(BF16)
    hi, mid, lo = _split3(x)
    return (jnp.dot(hi, m, preferred_element_type=F32) + jnp.dot(mid, m, preferred_element_type=F32)
            + jnp.dot(lo, m, preferred_element_type=F32))


def _dot_hp(a, b):
    a_hi = a.astype(BF16)
    a_lo = (a - a_hi.astype(F32)).astype(BF16)
    b_hi = b.astype(BF16)
    b_lo = (b - b_hi.astype(F32)).astype(BF16)
    return (jnp.dot(a_hi, b_hi, preferred_element_type=F32) + jnp.dot(a_lo, b_hi, preferred_element_type=F32)
            + jnp.dot(a_hi, b_lo, preferred_element_type=F32))


def _sigmoid(x):
    return 0.5 * jnp.tanh(0.5 * x) + 0.5


def _log_sigmoid(x):
    return jnp.minimum(x, 0.0) - jnp.log1p(jnp.exp(-jnp.abs(x)))


def _layer_norm(y, g, b, eps):
    mu = jnp.mean(y, axis=-1, keepdims=True)
    d = y - mu
    var = jnp.mean(d * d, axis=-1, keepdims=True)
    return d * lax.rsqrt(var + eps) * g + b


def _ffn_ln_kernel(x_ref, wg_ref, wu_ref, wd_ref, g_ref, b_ref, o_ref, ob_ref, h_ref, *, alpha, tf):
    x = x_ref[...]
    xb = x.astype(BF16)
    d_ff = wg_ref.shape[1]
    for c in range(d_ff // tf):
        sl = slice(c * tf, (c + 1) * tf)
        gate = jnp.dot(xb, wg_ref[:, sl], preferred_element_type=F32)
        up = jnp.dot(xb, wu_ref[:, sl], preferred_element_type=F32)
        h_ref[:, sl] = (gate * _sigmoid(gate) * up).astype(BF16)
    y = alpha * x + 0.5 * jnp.dot(h_ref[...], wd_ref[...], preferred_element_type=F32)
    out = _layer_norm(y, g_ref[...], b_ref[...], LN_EPS)
    o_ref[...] = out
    ob_ref[...] = out.astype(BF16)


def _ffn_ln(x, wg, wu, wd, g, b, alpha, tm=1024, tf=256):
    n, d = x.shape
    d_ff = wg.shape[1]
    const = lambda i: (0, 0)
    return pl.pallas_call(
        functools.partial(_ffn_ln_kernel, alpha=alpha, tf=tf),
        grid=(n // tm,),
        in_specs=[
            pl.BlockSpec((tm, d), lambda i: (i, 0)),
            pl.BlockSpec((d, d_ff), const, pipeline_mode=pl.Buffered(1)),
            pl.BlockSpec((d, d_ff), const, pipeline_mode=pl.Buffered(1)),
            pl.BlockSpec((d_ff, d), const, pipeline_mode=pl.Buffered(1)),
            pl.BlockSpec((1, d), const),
            pl.BlockSpec((1, d), const),
        ],
        out_specs=[pl.BlockSpec((tm, d), lambda i: (i, 0)), pl.BlockSpec((tm, d), lambda i: (i, 0))],
        out_shape=[jax.ShapeDtypeStruct((n, d), F32), jax.ShapeDtypeStruct((n, d), BF16)],
        scratch_shapes=[pltpu.VMEM((tm, d_ff), BF16)],
        compiler_params=pltpu.CompilerParams(dimension_semantics=("arbitrary",), vmem_limit_bytes=VMEM_LIMIT),
        name="ffn_ln",
    )(x, wg, wu, wd, g, b)


def _projm_kernel(x_ref, w_ref, cw_ref, cb_ref, gb_ref, ng_ref, of_ref, og_ref, hm_ref,
                  qkvo_s, gates_s, c_ref, n_ref, m_ref, pq_ref, pk_ref, *, nh, dh, L, pcols):
    tb = x_ref.shape[0]
    width = nh * dh
    n_m = qkvo_s.shape[2]
    n_f = of_ref.shape[1]
    n_g = og_ref.shape[1]
    halo = pq_ref.shape[0]
    step = pl.program_id(1)
    cur = lax.rem(step, jnp.int32(2))
    prv = 1 - cur

    @pl.when(step == 0)
    def _():
        qkvo_s[...] = jnp.zeros_like(qkvo_s)
        gates_s[...] = jnp.zeros_like(gates_s)
        pq_ref[...] = jnp.zeros_like(pq_ref)
        pk_ref[...] = jnp.zeros_like(pk_ref)

    @pl.when(step <= 1)
    def _():
        c_ref[...] = jnp.zeros_like(c_ref)
        n_ref[...] = jnp.zeros_like(n_ref)
        m_ref[...] = jnp.zeros_like(m_ref)

    def proj_task():
        x = x_ref[...]
        for c0 in range(0, n_m, pcols):
            c1 = min(c0 + pcols, n_m)
            qkvo_s[cur, :, c0:c1] = jnp.dot(x, w_ref[:, c0:c1], preferred_element_type=F32).astype(BF16)
            yield
        for c0 in range(0, n_f, pcols):
            c1 = min(c0 + pcols, n_f)
            res = jnp.dot(x, w_ref[:, n_m + c0:n_m + c1], preferred_element_type=F32)
            of_ref[:, c0:c1] = res
            if c1 == n_f:
                gates_s[cur] = res[:, c1 - c0 - 128:]
            yield
        for c0 in range(0, n_g, pcols):
            c1 = min(c0 + pcols, n_g)
            og_ref[:, c0:c1] = jnp.dot(x, w_ref[:, n_m + n_f + c0:n_m + n_f + c1],
                                       preferred_element_type=F32).astype(og_ref.dtype)
            yield

    sr = lax.broadcasted_iota(jnp.int32, ((M_CONV - 1) * L, halo + L), 0)
    sc = lax.broadcasted_iota(jnp.int32, ((M_CONV - 1) * L, halo + L), 1)
    shift = (sc == halo + sr % L - (sr // L + 1)).astype(BF16)
    half_w = 0.5 * cw_ref[...]
    half_b = 0.5 * cb_ref[...]
    ri = lax.broadcasted_iota(jnp.int32, (L, L), 0)
    ci = lax.broadcasted_iota(jnp.int32, (L, L), 1)
    causal = ri >= ci
    heads = range(nh)
    hs = [slice(h * dh, (h + 1) * dh) for h in heads]

    def mlstm_task(chunk):
        rows = slice(chunk * L, (chunk + 1) * L)

        def conv_silu(p_ref, w_off):
            x = qkvo_s[prv, rows, w_off:w_off + width]
            delayed = jnp.dot(shift, jnp.concatenate([p_ref[...], x], axis=0), preferred_element_type=F32)
            p_ref[...] = x[L - halo:, :]
            acc = half_b[:, w_off:w_off + width] + x.astype(F32) * half_w[M_CONV - 1:M_CONV, w_off:w_off + width]
            for j in range(1, M_CONV):
                acc = acc + delayed[(j - 1) * L:j * L] * half_w[M_CONV - 1 - j:M_CONV - j, w_off:w_off + width]
            return acc + acc * jnp.tanh(acc)

        q_all = conv_silu(pq_ref, 0) * (dh ** -0.5)
        yield
        k_all = conv_silu(pk_ref, width)
        yield
        gates = gates_s[prv, rows, :] + gb_ref[...]
        b_cols = _dot_exact_lhs(causal.astype(F32), _log_sigmoid(gates))
        gates_t = gates.T
        b_rows = b_cols.T
        yield

        q = [q_all[:, s] for s in hs]
        k = [k_all[:, s] for s in hs]
        v = [qkvo_s[prv, rows, 2 * width + h * dh:2 * width + (h + 1) * dh] for h in heads]
        i_col = [gates[:, h:h + 1] for h in heads]
        b_col = [b_cols[:, nh + h:nh + h + 1] for h in heads]
        i_row = [gates_t[h:h + 1, :] for h in heads]
        b_row = [b_rows[nh + h:nh + h + 1, :] for h in heads]
        g_tot = [b[L - 1:L, :] for b in b_col]
        c_prev = [c_ref[h] for h in heads]
        n_prev = [n_ref[8 * h:8 * h + 1, :] for h in heads]
        m_prev = [m_ref[8 * h:8 * h + 1, 0:1] for h in heads]

        a_col = [g_tot[h] - b_col[h] + i_col[h] for h in heads]
        m_loc = [jnp.max(z, axis=0, keepdims=True) for z in a_col]
        m_new = [jnp.maximum(g_tot[h] + m_prev[h], m_loc[h]) for h in heads]
        kw = [k[h] * jnp.exp(a_col[h] - m_new[h]) for h in heads]
        yield
        c_loc = [_bdot_tn(kw[h], v[h]) for h in heads]
        n_loc = [jnp.sum(z, axis=0, keepdims=True) for z in kw]
        yield

        d_log = [jnp.where(causal, b_col[h] - b_row[h] + i_row[h], -jnp.inf) for h in heads]
        inter = [b_col[h] + m_prev[h] for h in heads]
        m_t = [jnp.maximum(jnp.max(d_log[h], axis=-1, keepdims=True), inter[h]) for h in heads]
        yield
        s = [_bdot_nt(q[h], k[h]) * jnp.exp(d_log[h] - m_t[h]) for h in heads]
        yield
        s_inter = [jnp.exp(inter[h] - m_t[h]) for h in heads]
        num = [_bdot(s[h], v[h]) + s_inter[h] * _bdot(q[h], c_prev[h]) for h in heads]
        yield
        den = [jnp.sum(s[h], axis=-1, keepdims=True)
               + s_inter[h] * jnp.sum(q[h] * n_prev[h], axis=-1, keepdims=True) for h in heads]
        yield
        hid = [num[h] / jnp.maximum(jnp.abs(den[h]), jnp.exp(-m_t[h])) for h in heads]
        yield

        for h in heads:
            s_old = jnp.exp(g_tot[h] + m_prev[h] - m_new[h])
            c_ref[h] = s_old * c_prev[h] + c_loc[h]
            n_ref[8 * h:8 * h + 1, :] = s_old * n_prev[h] + n_loc[h]
            m_ref[8 * h:8 * h + 1, :] = jnp.broadcast_to(m_new[h], (1, m_ref.shape[1]))
        yield

        for h in heads:
            mu = jnp.mean(hid[h], axis=-1, keepdims=True)
            d = hid[h] - mu
            var = jnp.mean(d * d, axis=-1, keepdims=True)
            hn = d * lax.rsqrt(var + LN_EPS)
            gate_o = _sigmoid(qkvo_s[prv, rows, 3 * width + h * dh:3 * width + (h + 1) * dh].astype(F32))
            hm_ref[rows, hs[h]] = (hn * ng_ref[:, hs[h]] * gate_o).astype(hm_ref.dtype)
            yield

    def mlstm_all():
        for chunk in range(tb // L):
            yield from mlstm_task(chunk)

    _interleave(mlstm_all(), proj_task())


def _projm(xb, w, conv_w, conv_b, gate_bias, norm_g, n_f, bsz, t):
    n, d = xb.shape
    width = norm_g.shape[1]
    nh = M_HEADS
    n_m = 4 * width
    n_g = w.shape[1] - n_m - n_f
    tb = P_TILE
    nt = t // tb
    tile_in = lambda b, j: (b * nt + jnp.minimum(j, nt - 1), 0)
    tile_out = lambda b, j: (b * nt + jnp.maximum(j - 1, 0), 0)
    const = lambda b, j: (0, 0)
    return pl.pallas_call(
        functools.partial(_projm_kernel, nh=nh, dh=width // nh, L=M_CHUNK, pcols=P_COLS),
        grid=(bsz, nt + 1),
        in_specs=[
            pl.BlockSpec((tb, d), tile_in),
            pl.BlockSpec(w.shape, const, pipeline_mode=pl.Buffered(1)),
            pl.BlockSpec((M_CONV, 2 * width), const),
            pl.BlockSpec((1, 2 * width), const),
            pl.BlockSpec((1, 128), const),
            pl.BlockSpec((1, width), const),
        ],
        out_specs=[pl.BlockSpec((tb, n_f), tile_in), pl.BlockSpec((tb, n_g), tile_in),
                   pl.BlockSpec((tb, width), tile_out)],
        out_shape=[jax.ShapeDtypeStruct((n, n_f), F32), jax.ShapeDtypeStruct((n, n_g), BF16),
                   jax.ShapeDtypeStruct((n, width), BF16)],
        scratch_shapes=[
            pltpu.VMEM((2, tb, n_m), BF16),
            pltpu.VMEM((2, tb, 128), F32),
            pltpu.VMEM((nh, width // nh, width // nh), F32),
            pltpu.VMEM((8 * nh, width // nh), F32),
            pltpu.VMEM((8 * nh, 128), F32),
            pltpu.VMEM((M_HALO, width), BF16),
            pltpu.VMEM((M_HALO, width), BF16),
        ],
        compiler_params=pltpu.CompilerParams(dimension_semantics=("arbitrary", "arbitrary"),
                                             vmem_limit_bytes=VMEM_LIMIT),
        name="proj_mlstm",
    )(xb, w, conv_w, conv_b, gate_bias, norm_g)


def _tri_solve_cat(nms, cmm, L, out):
    shape = nms[0].shape
    ri = lax.broadcasted_iota(jnp.int32, shape, 0)
    ci = lax.broadcasted_iota(jnp.int32, shape, 1) % L
    blk = lambda w: (ri // w) == (ci // w)
    eye = (ri == ci).astype(F32)
    diag = blk(16)
    nds = [jnp.where(diag, nm, 0.0) for nm in nms]
    ts = [eye + nd for nd in nds]
    pws = [cmm(nd, nd) for nd in nds]
    yield
    for _ in range(2):
        both = [cmm(jnp.concatenate([p, t], axis=0), p) for p, t in zip(pws, ts)]
        pws = [b[:L] for b in both]
        ts = [t + b[L:] for t, b in zip(ts, both)]
        yield
    ts = [t + cmm(t, p) for t, p in zip(ts, pws)]
    yield
    ys = [cmm(jnp.where(diag, 0.0, nm), t) for nm, t in zip(nms, ts)]
    yield
    both = [cmm(jnp.concatenate([y, t], axis=0), y) for y, t in zip(ys, ts)]
    y2s = [b[:L] for b in both]
    ts = [t + b[L:] for t, b in zip(ts, both)]
    yield
    out["t"] = [t + cmm(t, y2) for t, y2 in zip(ts, y2s)]


def _interleave(*tasks):
    tasks = list(tasks)
    while tasks:
        for task in list(tasks):
            try:
                next(task)
            except StopIteration:
                tasks.remove(task)


def _rwkv_kernel(xr_ref, xk_ref, xv_ref, xl_ref, mur_ref, muk_ref, muv_ref, mul_ref,
                 w0_ref, a0_ref, kkw_ref, kaw_ref, rrk_ref, gng_ref, gnb_ref,
                 w2_ref, a2_ref, g2_ref, out_ref,
                 s_ref, pr_ref, pk_ref, pv_ref, pl_ref, *, L, nchunk, hd, gw):
    R = xr_ref.shape[0]
    width = xr_ref.shape[1]
    ngroups = width // gw
    hpg = gw // hd
    step = pl.program_id(1)

    @pl.when(step == 0)
    def _():
        s_ref[...] = jnp.zeros_like(s_ref)
        pr_ref[...] = jnp.zeros_like(pr_ref)
        pk_ref[...] = jnp.zeros_like(pk_ref)
        pv_ref[...] = jnp.zeros_like(pv_ref)
        pl_ref[...] = jnp.zeros_like(pl_ref)

    row_r = lax.broadcasted_iota(jnp.int32, (R, 1), 0)
    row_l = lax.broadcasted_iota(jnp.int32, (L, 1), 0)
    carry = {}
    for key, x_ref, p_ref in (("r", xr_ref, pr_ref), ("k", xk_ref, pk_ref), ("v", xv_ref, pv_ref),
                              ("l", xl_ref, pl_ref)):
        carry[key] = p_ref[0:1, :]
        p_ref[0:1, :] = x_ref[R - 1:R, :]

    def shift_mix(x_ref, key, mu_ref, c, cs):
        x = x_ref[c * L:(c + 1) * L, cs]
        first = carry[key][:, cs] if c == 0 else x_ref[c * L - 1:c * L, cs]
        prev = jnp.where(row_l == 0, first, pltpu.roll(x, 1, 0))
        return x + (prev - x) * mu_ref[:, cs]

    li = lax.broadcasted_iota(jnp.int32, (gw, gw), 0)
    lj = lax.broadcasted_iota(jnp.int32, (gw, gw), 1)
    same_head = (li // hd) == (lj // hd)
    lane_head = lax.broadcasted_iota(jnp.int32, (1, gw), 1) // hd

    def seg_sums(zs):
        outs = []
        for z in zs:
            acc = jnp.zeros_like(z)
            for h in range(hpg):
                mine = lane_head == h
                acc = jnp.where(mine, jnp.sum(jnp.where(mine, z, 0.0), axis=-1, keepdims=True), acc)
            outs.append(acc)
        return outs

    def stack(z):
        zb = z.astype(BF16)
        return jnp.where(same_head, jnp.concatenate([zb] * hpg, axis=0), jnp.zeros((), BF16))

    def cmm(a_cat, b):
        return jnp.dot(a_cat.astype(BF16), stack(b), preferred_element_type=F32)

    ti = lax.broadcasted_iota(jnp.int32, (L, L), 0)
    tj = lax.broadcasted_iota(jnp.int32, (L, L), 1)
    tri = (ti >= tj).astype(F32)
    ci_ = lax.broadcasted_iota(jnp.int32, (L, gw), 0)
    cj_ = lax.broadcasted_iota(jnp.int32, (L, gw), 1) % L
    strict = ci_ > cj_
    incl = ci_ >= cj_

    groups = range(ngroups)
    cols = [slice(grp * gw, (grp + 1) * gw) for grp in groups]

    xl = xl_ref[...]
    xl_prev = jnp.where(row_r == 0, carry["l"], pltpu.roll(xl, 1, 0))
    xl = xl + (xl_prev - xl) * mul_ref[...]
    xl_tanh = jnp.tanh(xl)
    xl_sig = _sigmoid(xl)
    wl_pre = [w0_ref[:, cs] + _dot_hp(xl_tanh, w2_ref[:, cs]) for cs in cols]
    a_pre = [a0_ref[:, cs] + _bdot(xl, a2_ref[:, cs]) for cs in cols]
    gate = [_bdot(xl_sig, g2_ref[:, cs]) for cs in cols]

    ctx = [dict() for _ in range(nchunk)]

    def setup_task(c):
        p = ctx[c]
        rs = slice(c * L, (c + 1) * L)
        rr = [shift_mix(xr_ref, "r", mur_ref, c, cs) for cs in cols]
        yield
        rk = [shift_mix(xk_ref, "k", muk_ref, c, cs) for cs in cols]
        yield
        rv = [shift_mix(xv_ref, "v", muv_ref, c, cs) for cs in cols]
        p["v"] = rv
        yield
        lw = [-jnp.exp(_log_sigmoid(z[rs]) - 0.5) for z in wl_pre]
        yield
        a = [_sigmoid(z[rs]) for z in a_pre]
        kk = [x * kkw_ref[:, cs] for x, cs in zip(rk, cols)]
        k2 = [x * (1.0 + (ai - 1.0) * kaw_ref[:, cs]) for x, ai, cs in zip(rk, a, cols)]
        yield
        sums = seg_sums([x * x for x in kk] + [r * k * rrk_ref[:, cs] for r, k, cs in zip(rr, k2, cols)])
        yield
        kk = [x / jnp.maximum(jnp.sqrt(s), 1e-12) for x, s in zip(kk, sums[:ngroups])]
        bv = [x * ai for x, ai in zip(kk, a)]
        p["bonus"] = [s * v for s, v in zip(sums[ngroups:], rv)]
        yield
        cum_c = [_dot_exact_lhs(tri, z) for z in lw]
        cum_l = [z[L - 1:L, :] for z in cum_c]
        p["w_l"] = [jnp.exp(z) for z in cum_l]
        yield
        e_neg = [jnp.exp(-z) for z in cum_c]
        yield
        bk_stack = [jnp.concatenate([stack(b * e), stack(k * e)], axis=0) for b, k, e in zip(bv, k2, e_neg)]
        yield
        at = [-x * jnp.exp(z - w) for x, z, w in zip(kk, cum_c, lw)]
        yield
        rt = [x * jnp.exp(z) for x, z in zip(rr, cum_c)]
        p["ar"] = [jnp.concatenate([x, y], axis=0) for x, y in zip(at, rt)]
        yield
        aa = [_bdot_nt(x, y) for x, y in zip(p["ar"], bk_stack)]
        yield
        e_end = [jnp.exp(zl - z) for zl, z in zip(cum_l, cum_c)]
        p["bk_end"] = [jnp.concatenate([b * e, k * e], axis=0) for b, k, e in zip(bv, k2, e_end)]
        yield
        p["nm"] = [jnp.where(strict, z[:L, :gw], 0.0) for z in aa]
        p["arb"] = [jnp.where(incl, z[L:, :gw], 0.0) for z in aa]
        yield
        ak = [jnp.concatenate([jnp.where(strict, z[:L, gw:], 0.0), jnp.where(incl, z[L:, gw:], 0.0)], axis=0)
              for z in aa]
        yield
        p["akv"] = [cmm(x, v) for x, v in zip(ak, p["v"])]

    def solve_task(c):
        yield from _tri_solve_cat(ctx[c]["nm"], cmm, L, ctx[c])

    def tail_task(c):
        p = ctx[c]
        rs = slice(c * L, (c + 1) * L)
        s0 = [s_ref[grp] for grp in groups]
        ps = [_bdot_nt(x, s) for x, s in zip(p["ar"], s0)]
        yield
        u = [cmm(t, z[:L] + w[:L]) for t, z, w in zip(p["t"], ps, p["akv"])]
        yield
        y = [z[L:] + cmm(x, ui) + w[L:] for z, x, ui, w in zip(ps, p["arb"], u, p["akv"])]
        yield
        upd = [_bdot_tn(jnp.concatenate([ui, v], axis=0), x) for ui, v, x in zip(u, p["v"], p["bk_end"])]
        yield
        for grp in groups:
            s_ref[grp] = s0[grp] * p["w_l"][grp] + jnp.where(same_head, upd[grp], 0.0)
        yield
        mu = [z * (1.0 / hd) for z in seg_sums(y)]
        yield
        d = [z - m for z, m in zip(y, mu)]
        var = [z * (1.0 / hd) for z in seg_sums([z * z for z in d])]
        yield
        for grp in groups:
            cs = cols[grp]
            yn = d[grp] * lax.rsqrt(var[grp] + R_GN_EPS) * gng_ref[:, cs] + gnb_ref[:, cs]
            out_ref[rs, cs] = ((yn + p["bonus"][grp]) * gate[grp][rs]).astype(out_ref.dtype)

    for k in range(nchunk + 2):
        tasks = []
        if 0 <= k - 2 < nchunk:
            tasks.append(tail_task(k - 2))
        if 0 <= k - 1 < nchunk:
            tasks.append(solve_task(k - 1))
        if k < nchunk:
            tasks.append(setup_task(k))
        _interleave(*tasks)


def _rwkv(pr, r_mu, w0, a0, kkw, kaw, rrk, gng, gnb, w2p, a2p, g2p, bsz, t):
    n = pr.shape[0]
    width = w0.shape[1]
    hd = R_HEAD_DIM
    gw = R_GROUP_HEADS * hd
    L = R_CHUNK
    R = L * R_CHUNKS_PER_STEP
    ns = t // R
    lr = w2p.shape[0]
    lblk = (3 * width) // lr
    rows = lambda b, c: b * ns + c
    xspec = lambda off: pl.BlockSpec((R, width), lambda b, c: (rows(b, c), off))
    pspec = lambda off: pl.BlockSpec((1, width), lambda b, c: (0, off))
    gspec = pl.BlockSpec((1, width), lambda b, c: (0, 0))
    wspec = pl.BlockSpec((lr, width), lambda b, c: (0, 0))
    return pl.pallas_call(
        functools.partial(_rwkv_kernel, L=L, nchunk=R_CHUNKS_PER_STEP, hd=hd, gw=gw),
        grid=(bsz, ns),
        in_specs=[
            xspec(0), xspec(1), xspec(2),
            pl.BlockSpec((R, lr), lambda b, c: (rows(b, c), lblk)),
            pspec(0), pspec(1), pspec(2),
            pl.BlockSpec((1, lr), lambda b, c: (0, lblk)),
            gspec, gspec, gspec, gspec, gspec, gspec, gspec,
            wspec, wspec, wspec,
        ],
        out_specs=pl.BlockSpec((R, width), lambda b, c: (rows(b, c), 0)),
        out_shape=jax.ShapeDtypeStruct((n, width), BF16),
        scratch_shapes=[
            pltpu.VMEM((width // gw, gw, gw), F32),
            pltpu.VMEM((8, width), F32),
            pltpu.VMEM((8, width), F32),
            pltpu.VMEM((8, width), F32),
            pltpu.VMEM((8, lr), F32),
        ],
        compiler_params=pltpu.CompilerParams(dimension_semantics=("arbitrary", "arbitrary"),
                                             vmem_limit_bytes=VMEM_LIMIT),
        name="rwkv",
    )(pr, pr, pr, pr, r_mu, r_mu, r_mu, r_mu, w0, a0, kkw, kaw, rrk, gng, gnb, w2p, a2p, g2p)


def _post_kernel(hm_ref, yr_ref, ga_ref, gb_ref, x_ref, wa_ref, wb_ref, wo_ref, g_ref, b_ref, o_ref, *, alpha):
    ya = jnp.dot(hm_ref[...], wa_ref[...], preferred_element_type=F32)
    yb = jnp.dot(yr_ref[...], wb_ref[...], preferred_element_type=F32)
    merged = _sigmoid(ga_ref[...].astype(F32)) * ya + _sigmoid(gb_ref[...].astype(F32)) * yb
    mix = jnp.dot(merged.astype(BF16), wo_ref[...], preferred_element_type=F32)
    o_ref[...] = _layer_norm(alpha * x_ref[...] + mix, g_ref[...], b_ref[...], LN_EPS)


def _post(hm, yr, pgate, x1, wa, wb, wo, g, b, alpha, tm=512):
    n, d = x1.shape
    tile = lambda j: pl.BlockSpec((tm, d), lambda i: (i, j))
    wspec = pl.BlockSpec((d, d), lambda i: (0, 0), pipeline_mode=pl.Buffered(1))
    vspec = pl.BlockSpec((1, d), lambda i: (0, 0))
    return pl.pallas_call(
        functools.partial(_post_kernel, alpha=alpha),
        grid=(n // tm,),
        in_specs=[tile(0), tile(0), tile(0), tile(1), tile(0), wspec, wspec, wspec, vspec, vspec],
        out_specs=tile(0),
        out_shape=jax.ShapeDtypeStruct((n, d), F32),
        compiler_params=pltpu.CompilerParams(dimension_semantics=("arbitrary",), vmem_limit_bytes=VMEM_LIMIT),
        name="post",
    )(hm, yr, pgate, pgate, x1, wa, wb, wo, g, b)


def kernel(x, ffn1_w_gate, ffn1_w_up, ffn1_w_down, ln1_g, ln1_b, w_in, m_conv_w, m_conv_b, m_i_bias, m_f_bias,
           m_norm_g, r_mu, r_w0, r_w2, r_a0, r_a2, r_g2, r_k_k, r_k_a, r_r_k, r_gn_g, r_gn_b, w_branch_a,
           w_branch_b, w_out, ln2_g, ln2_b, ffn2_w_gate, ffn2_w_up, ffn2_w_down, ln3_g, ln3_b):
    bsz, t, d = x.shape
    depth = w_in.shape[0]
    alpha = (2 * depth) ** 0.25
    mw = d
    rw = d
    nh = M_HEADS
    lr = R_DECAY_RANK + R_AAA_RANK + R_GATE_RANK
    row = lambda p: p.reshape(1, -1)

    cur = x.reshape(bsz * t, d)
    for l in range(depth):
        x1, x1b = _ffn_ln(cur, ffn1_w_gate[l].astype(BF16), ffn1_w_up[l].astype(BF16),
                          ffn1_w_down[l].astype(BF16), row(ln1_g[l]), row(ln1_b[l]), alpha)

        w = w_in[l]
        o_gate = 4 * mw
        o_rc = o_gate + 2 * nh
        o_ga = o_rc + 3 * rw + lr
        w_all = jnp.concatenate(
            [w[:, :o_gate], w[:, o_rc:o_ga], jnp.pad(w[:, o_gate:o_rc], ((0, 0), (0, 128 - 2 * nh))), w[:, o_ga:]],
            axis=1).astype(BF16)
        gate_bias = jnp.pad(jnp.concatenate([m_i_bias[l], m_f_bias[l]]), (0, 128 - 2 * nh)).reshape(1, 128)
        pr, pgate, hm = _projm(x1b, w_all, m_conv_w[l], row(m_conv_b[l]), gate_bias, row(m_norm_g[l]),
                               o_ga - o_rc + 128, bsz, t)

        zeros = lambda r: jnp.zeros((r, rw), F32)
        w2p = jnp.concatenate([r_w2[l], zeros(R_AAA_RANK + R_GATE_RANK)], axis=0)
        a2p = jnp.concatenate([zeros(R_DECAY_RANK), r_a2[l], zeros(R_GATE_RANK)], axis=0)
        g2p = jnp.concatenate([zeros(R_DECAY_RANK + R_AAA_RANK), r_g2[l]], axis=0)
        yr = _rwkv(pr, row(r_mu[l]), row(r_w0[l]), row(r_a0[l]), row(r_k_k[l]), row(r_k_a[l]), row(r_r_k[l]),
                   row(r_gn_g[l]), row(r_gn_b[l]), w2p, a2p, g2p, bsz, t)

        x2 = _post(hm, yr, pgate, x1, w_branch_a[l].astype(BF16), w_branch_b[l].astype(BF16),
                   w_out[l].astype(BF16), row(ln2_g[l]), row(ln2_b[l]), alpha)
        cur, _ = _ffn_ln(x2, ffn2_w_gate[l].astype(BF16), ffn2_w_up[l].astype(BF16),
                         ffn2_w_down[l].astype(BF16), row(ln3_g[l]), row(ln3_b[l]), alpha)
    return cur.reshape(bsz, t, d)
```

```python
import functools

import jax
import jax.numpy as jnp
from jax import lax
from jax.experimental import pallas as pl
from jax.experimental.pallas import tpu as pltpu

F32 = jnp.float32
BF16 = jnp.bfloat16

M_HEADS = 4
M_CONV = 4
M_CHUNK = 128
M_HALO = 16
R_HEAD_DIM = 64
R_DECAY_RANK = 64
R_AAA_RANK = 64
R_GATE_RANK = 128
R_GN_EPS = 64e-5
LN_EPS = 1e-5

R_CHUNK = 64
R_GROUP_HEADS = 2
R_CHUNKS_PER_STEP = 8
P_TILE = 256
P_COLS = 256
VMEM_LIMIT = 56 * 1024 * 1024


def _bdot(a, b):
    return jnp.dot(a.astype(BF16), b.astype(BF16), preferred_element_type=F32)


def _bdot_nt(a, b):
    return lax.dot_general(a.astype(BF16), b.astype(BF16), (((1,), (1,)), ((), ())),
                           preferred_element_type=F32)


def _bdot_tn(a, b):
    return lax.dot_general(a.astype(BF16), b.astype(BF16), (((0,), (0,)), ((), ())),
                           preferred_element_type=F32)


def _split3(x):
    hi = x.astype(BF16)
    r1 = x - hi.astype(F32)
    mid = r1.astype(BF16)
    lo = (r1 - mid.astype(F32)).astype(BF16)
    return hi, mid, lo


def _dot_exact_lhs(m01, x):
    m = m01.astype(BF16)
    hi, mid, lo = _split3(x)
    return (jnp.dot(m, hi, preferred_element_type=F32) + jnp.dot(m, mid, preferred_element_type=F32)
            + jnp.dot(m, lo, preferred_element_type=F32))


def _dot_hp(a, b):
    a_hi = a.astype(BF16)
    a_lo = (a - a_hi.astype(F32)).astype(BF16)
    b_hi = b.astype(BF16)
    b_lo = (b - b_hi.astype(F32)).astype(BF16)
    return (jnp.dot(a_hi, b_hi, preferred_element_type=F32) + jnp.dot(a_lo, b_hi, preferred_element_type=F32)
            + jnp.dot(a_hi, b_lo, preferred_element_type=F32))


def _sigmoid(x):
    return 0.5 * jnp.tanh(0.5 * x) + 0.5


def _log_sigmoid(x):
    return jnp.minimum(x, 0.0) - jnp.log1p(jnp.exp(-jnp.abs(x)))


def _layer_norm(y, g, b, eps):
    mu = jnp.mean(y, axis=-1, keepdims=True)
    d = y - mu
    var = jnp.mean(d * d, axis=-1, keepdims=True)
    return d * lax.rsqrt(var + eps) * g + b


def _ffn_ln_kernel(x_ref, wg_ref, wu_ref, wd_ref, g_ref, b_ref, o_ref, ob_ref, h_ref, *, alpha, tf):
    x = x_ref[...]
    xb = x.astype(BF16)
    d_ff = wg_ref.shape[1]
    for c in range(d_ff // tf):
        sl = slice(c * tf, (c + 1) * tf)
        gate = jnp.dot(xb, wg_ref[:, sl], preferred_element_type=F32)
        up = jnp.dot(xb, wu_ref[:, sl], preferred_element_type=F32)
        h_ref[:, sl] = (gate * _sigmoid(gate) * up).astype(BF16)
    y = alpha * x + 0.5 * jnp.dot(h_ref[...], wd_ref[...], preferred_element_type=F32)
    out = _layer_norm(y, g_ref[...], b_ref[...], LN_EPS)
    o_ref[...] = out
    ob_ref[...] = out.astype(BF16)


def _ffn_ln(x, wg, wu, wd, g, b, alpha, tm=1024, tf=256):
    n, d = x.shape
    d_ff = wg.shape[1]
    const = lambda i: (0, 0)
    return pl.pallas_call(
        functools.partial(_ffn_ln_kernel, alpha=alpha, tf=tf),
        grid=(n // tm,),
        in_specs=[
            pl.BlockSpec((tm, d), lambda i: (i, 0)),
            pl.BlockSpec((d, d_ff), const, pipeline_mode=pl.Buffered(1)),
            pl.BlockSpec((d, d_ff), const, pipeline_mode=pl.Buffered(1)),
            pl.BlockSpec((d_ff, d), const, pipeline_mode=pl.Buffered(1)),
            pl.BlockSpec((1, d), const),
            pl.BlockSpec((1, d), const),
        ],
        out_specs=[pl.BlockSpec((tm, d), lambda i: (i, 0)), pl.BlockSpec((tm, d), lambda i: (i, 0))],
        out_shape=[jax.ShapeDtypeStruct((n, d), F32), jax.ShapeDtypeStruct((n, d), BF16)],
        scratch_shapes=[pltpu.VMEM((tm, d_ff), BF16)],
        compiler_params=pltpu.CompilerParams(dimension_semantics=("arbitrary",), vmem_limit_bytes=VMEM_LIMIT),
        name="ffn_ln",
    )(x, wg, wu, wd, g, b)


def _projm_kernel(x_ref, wm_ref, wr_ref, wg_ref, cw_ref, cb_ref, gb_ref, ng_ref, of_ref, og_ref, hm_ref,
                  qkvo_s, gates_s, c_ref, n_ref, m_ref, pq_ref, pk_ref, *, nh, dh, L, pcols):
    tb = x_ref.shape[0]
    width = nh * dh
    n_m = qkvo_s.shape[2]
    n_f = of_ref.shape[1]
    n_g = og_ref.shape[1]
    halo = pq_ref.shape[0]
    step = pl.program_id(1)
    cur = lax.rem(step, jnp.int32(2))
    prv = 1 - cur

    @pl.when(step == 0)
    def _():
        pq_ref[...] = jnp.zeros_like(pq_ref)
        pk_ref[...] = jnp.zeros_like(pk_ref)
        c_ref[...] = jnp.zeros_like(c_ref)
        n_ref[...] = jnp.zeros_like(n_ref)
        m_ref[...] = jnp.zeros_like(m_ref)

    def proj_task():
        x = x_ref[...]
        for c0 in range(0, n_m, pcols):
            c1 = min(c0 + pcols, n_m)
            qkvo_s[cur, :, c0:c1] = jnp.dot(x, wm_ref[:, c0:c1], preferred_element_type=F32).astype(BF16)
            yield
        n_r = n_f - 128
        for c0 in range(0, n_r, pcols):
            c1 = min(c0 + pcols, n_r)
            of_ref[:, c0:c1] = jnp.dot(x, wr_ref[:, c0:c1], preferred_element_type=F32)
            yield
        gates = jnp.dot(x, wg_ref[...], preferred_element_type=F32)
        of_ref[:, n_r:] = gates
        gates_s[cur] = gates
        yield
        for c0 in range(0, n_g, pcols):
            c1 = min(c0 + pcols, n_g)
            og_ref[:, c0:c1] = jnp.dot(x, wr_ref[:, n_r + c0:n_r + c1],
                                       preferred_element_type=F32).astype(og_ref.dtype)
            yield

    sr = lax.broadcasted_iota(jnp.int32, ((M_CONV - 1) * L, halo + L), 0)
    sc = lax.broadcasted_iota(jnp.int32, ((M_CONV - 1) * L, halo + L), 1)
    shift = (sc == halo + sr % L - (sr // L + 1)).astype(BF16)
    half_w = 0.5 * cw_ref[...]
    half_b = 0.5 * cb_ref[...]
    ri = lax.broadcasted_iota(jnp.int32, (L, L), 0)
    ci = lax.broadcasted_iota(jnp.int32, (L, L), 1)
    causal = ri >= ci
    heads = range(nh)
    hs = [slice(h * dh, (h + 1) * dh) for h in heads]

    def mlstm_task(chunk):
        rows = slice(chunk * L, (chunk + 1) * L)

        def conv_silu(p_ref, w_off):
            x = qkvo_s[prv, rows, w_off:w_off + width]
            delayed = jnp.dot(shift, jnp.concatenate([p_ref[...], x], axis=0), preferred_element_type=F32)
            p_ref[...] = x[L - halo:, :]
            acc = half_b[:, w_off:w_off + width] + x.astype(F32) * half_w[M_CONV - 1:M_CONV, w_off:w_off + width]
            for j in range(1, M_CONV):
                acc = acc + delayed[(j - 1) * L:j * L] * half_w[M_CONV - 1 - j:M_CONV - j, w_off:w_off + width]
            return acc + acc * jnp.tanh(acc)

        q_all = conv_silu(pq_ref, 0) * (dh ** -0.5)
        yield
        k_all = conv_silu(pk_ref, width)
        yield
        gates = gates_s[prv, rows, :] + gb_ref[...]
        b_cols = _dot_exact_lhs(causal.astype(F32), _log_sigmoid(gates))
        gates_t = gates.T
        b_rows = b_cols.T
        yield

        q = [q_all[:, s] for s in hs]
        k = [k_all[:, s] for s in hs]
        v = [qkvo_s[prv, rows, 2 * width + h * dh:2 * width + (h + 1) * dh] for h in heads]
        i_col = [gates[:, h:h + 1] for h in heads]
        b_col = [b_cols[:, nh + h:nh + h + 1] for h in heads]
        i_row = [gates_t[h:h + 1, :] for h in heads]
        b_row = [b_rows[nh + h:nh + h + 1, :] for h in heads]
        g_tot = [b[L - 1:L, :] for b in b_col]
        c_prev = [c_ref[h] for h in heads]
        n_prev = [n_ref[8 * h:8 * h + 1, :] for h in heads]
        m_prev = [m_ref[8 * h:8 * h + 1, 0:1] for h in heads]

        a_col = [g_tot[h] - b_col[h] + i_col[h] for h in heads]
        m_loc = [jnp.max(z, axis=0, keepdims=True) for z in a_col]
        m_new = [jnp.maximum(g_tot[h] + m_prev[h], m_loc[h]) for h in heads]
        kw = [k[h] * jnp.exp(a_col[h] - m_new[h]) for h in heads]
        yield
        c_loc = [_bdot_tn(kw[h], v[h]) for h in heads]
        n_loc = [jnp.sum(z, axis=0, keepdims=True) for z in kw]
        yield

        d_log = [jnp.where(causal, b_col[h] - b_row[h] + i_row[h], -jnp.inf) for h in heads]
        inter = [b_col[h] + m_prev[h] for h in heads]
        m_t = [jnp.maximum(jnp.max(d_log[h], axis=-1, keepdims=True), inter[h]) for h in heads]
        yield
        s = [_bdot_nt(q[h], k[h]) * jnp.exp(d_log[h] - m_t[h]) for h in heads]
        yield
        s_inter = [jnp.exp(inter[h] - m_t[h]) for h in heads]
        num = [_bdot(s[h], v[h]) + s_inter[h] * _bdot(q[h], c_prev[h]) for h in heads]
        yield
        den = [jnp.sum(s[h], axis=-1, keepdims=True)
               + s_inter[h] * jnp.sum(q[h] * n_prev[h], axis=-1, keepdims=True) for h in heads]
        yield
        hid = [num[h] / jnp.maximum(jnp.abs(den[h]), jnp.exp(-m_t[h])) for h in heads]
        yield

        for h in heads:
            s_old = jnp.exp(g_tot[h] + m_prev[h] - m_new[h])
            c_ref[h] = s_old * c_prev[h] + c_loc[h]
            n_ref[8 * h:8 * h + 1, :] = s_old * n_prev[h] + n_loc[h]
            m_ref[8 * h:8 * h + 1, :] = jnp.broadcast_to(m_new[h], (1, m_ref.shape[1]))
        yield

        for h in heads:
            mu = jnp.mean(hid[h], axis=-1, keepdims=True)
            d = hid[h] - mu
            var = jnp.mean(d * d, axis=-1, keepdims=True)
            hn = d * lax.rsqrt(var + LN_EPS)
            gate_o = _sigmoid(qkvo_s[prv, rows, 3 * width + h * dh:3 * width + (h + 1) * dh].astype(F32))
            hm_ref[rows, hs[h]] = (hn * ng_ref[:, hs[h]] * gate_o).astype(hm_ref.dtype)
            yield

    def mlstm_all():
        for chunk in range(tb // L):
            yield from mlstm_task(chunk)

    last = pl.num_programs(1) - 1

    @pl.when(step == 0)
    def _():
        _interleave(proj_task())

    @pl.when(jnp.logical_and(step > 0, step < last))
    def _():
        _interleave(mlstm_all(), proj_task())

    @pl.when(step == last)
    def _():
        _interleave(mlstm_all())


def _projm(xb, w_full, w_rest, w_gates, conv_w, conv_b, gate_bias, norm_g, n_g, bsz, t):
    n, d = xb.shape
    width = norm_g.shape[1]
    nh = M_HEADS
    n_m = 4 * width
    n_f = w_rest.shape[1] - n_g + 128
    tb = P_TILE
    nt = t // tb
    tile_in = lambda b, j: (b * nt + jnp.minimum(j, nt - 1), 0)
    tile_out = lambda b, j: (b * nt + jnp.maximum(j - 1, 0), 0)
    const = lambda b, j: (0, 0)
    return pl.pallas_call(
        functools.partial(_projm_kernel, nh=nh, dh=width // nh, L=M_CHUNK, pcols=P_COLS),
        grid=(bsz, nt + 1),
        in_specs=[
            pl.BlockSpec((tb, d), tile_in),
            pl.BlockSpec((d, n_m), const, pipeline_mode=pl.Buffered(1)),
            pl.BlockSpec(w_rest.shape, const, pipeline_mode=pl.Buffered(1)),
            pl.BlockSpec(w_gates.shape, const, pipeline_mode=pl.Buffered(1)),
            pl.BlockSpec((M_CONV, 2 * width), const),
            pl.BlockSpec((1, 2 * width), const),
            pl.BlockSpec((1, 128), const),
            pl.BlockSpec((1, width), const),
        ],
        out_specs=[pl.BlockSpec((tb, n_f), tile_in), pl.BlockSpec((tb, n_g), tile_in),
                   pl.BlockSpec((tb, width), tile_out)],
        out_shape=[jax.ShapeDtypeStruct((n, n_f), F32), jax.ShapeDtypeStruct((n, n_g), BF16),
                   jax.ShapeDtypeStruct((n, width), BF16)],
        scratch_shapes=[
            pltpu.VMEM((2, tb, n_m), BF16),
            pltpu.VMEM((2, tb, 128), F32),
            pltpu.VMEM((nh, width // nh, width // nh), F32),
            pltpu.VMEM((8 * nh, width // nh), F32),
            pltpu.VMEM((8 * nh, 128), F32),
            pltpu.VMEM((M_HALO, width), BF16),
            pltpu.VMEM((M_HALO, width), BF16),
        ],
        compiler_params=pltpu.CompilerParams(dimension_semantics=("arbitrary", "arbitrary"),
                                             vmem_limit_bytes=VMEM_LIMIT),
        name="proj_mlstm",
    )(xb, w_full, w_rest, w_gates, conv_w, conv_b, gate_bias, norm_g)


def _tri_solve_cat(nms, cmm, L, out):
    shape = nms[0].shape
    ri = lax.broadcasted_iota(jnp.int32, shape, 0)
    ci = lax.broadcasted_iota(jnp.int32, shape, 1) % L
    blk = lambda w: (ri // w) == (ci // w)
    eye = (ri == ci).astype(F32)
    diag = blk(16)
    nds = [jnp.where(diag, nm, 0.0) for nm in nms]
    ts = [eye + nd for nd in nds]
    pws = [cmm(nd, nd) for nd in nds]
    yield
    for _ in range(2):
        both = [cmm(jnp.concatenate([p, t], axis=0), p) for p, t in zip(pws, ts)]
        pws = [b[:L] for b in both]
        ts = [t + b[L:] for t, b in zip(ts, both)]
        yield
    ts = [t + cmm(t, p) for t, p in zip(ts, pws)]
    yield
    ys = [cmm(jnp.where(diag, 0.0, nm), t) for nm, t in zip(nms, ts)]
    yield
    power = 1
    while 2 * power < L // 16:
        both = [cmm(jnp.concatenate([y, t], axis=0), y) for y, t in zip(ys, ts)]
        ys = [b[:L] for b in both]
        ts = [t + b[L:] for t, b in zip(ts, both)]
        power *= 2
        yield
    out["t"] = [t + cmm(t, y) for t, y in zip(ts, ys)]


def _interleave(*tasks):
    tasks = list(tasks)
    while tasks:
        for task in list(tasks):
            try:
                next(task)
            except StopIteration:
                tasks.remove(task)


def _rwkv_kernel(xr_ref, xk_ref, xv_ref, xl_ref, mur_ref, muk_ref, muv_ref, mul_ref,
                 w0_ref, a0_ref, kkw_ref, kaw_ref, rrk_ref, gng_ref, gnb_ref,
                 w2_ref, a2_ref, g2_ref, out_ref,
                 s_ref, pr_ref, pk_ref, pv_ref, pl_ref, *, L, nchunk, hd, gw):
    R = xr_ref.shape[0]
    width = xr_ref.shape[1]
    ngroups = width // gw
    hpg = gw // hd
    step = pl.program_id(1)

    @pl.when(step == 0)
    def _():
        s_ref[...] = jnp.zeros_like(s_ref)
        pr_ref[...] = jnp.zeros_like(pr_ref)
        pk_ref[...] = jnp.zeros_like(pk_ref)
        pv_ref[...] = jnp.zeros_like(pv_ref)
        pl_ref[...] = jnp.zeros_like(pl_ref)

    row_r = lax.broadcasted_iota(jnp.int32, (R, 1), 0)
    row_l = lax.broadcasted_iota(jnp.int32, (L, 1), 0)
    carry = {}
    for key, x_ref, p_ref in (("r", xr_ref, pr_ref), ("k", xk_ref, pk_ref), ("v", xv_ref, pv_ref),
                              ("l", xl_ref, pl_ref)):
        carry[key] = p_ref[0:1, :]
        p_ref[0:1, :] = x_ref[R - 1:R, :]

    def shift_mix(x_ref, key, mu_ref, c, cs):
        x = x_ref[c * L:(c + 1) * L, cs]
        first = carry[key][:, cs] if c == 0 else x_ref[c * L - 1:c * L, cs]
        prev = jnp.where(row_l == 0, first, pltpu.roll(x, 1, 0))
        return x + (prev - x) * mu_ref[:, cs]

    catw = hpg * L
    li = lax.broadcasted_iota(jnp.int32, (gw, gw), 0)
    lj = lax.broadcasted_iota(jnp.int32, (gw, gw), 1)
    same_head = (li // hd) == (lj // hd)
    si = lax.broadcasted_iota(jnp.int32, (catw, 1), 0) // L
    stack_full = si == lax.broadcasted_iota(jnp.int32, (1, gw), 1) // hd
    stack_cat = si == lax.broadcasted_iota(jnp.int32, (1, catw), 1) // L
    lane_head = lax.broadcasted_iota(jnp.int32, (1, gw), 1) // hd

    def seg_sums(zs):
        outs = []
        for z in zs:
            sums = [jnp.sum(jnp.where(lane_head == h, z, 0.0), axis=-1, keepdims=True) for h in range(hpg)]
            acc = jnp.broadcast_to(sums[hpg - 1], z.shape)
            for h in range(hpg - 1):
                acc = jnp.where(lane_head == h, sums[h], acc)
            outs.append(acc)
        return outs

    def stack(z):
        zb = z.astype(BF16)
        mask = stack_full if z.shape[1] == gw else stack_cat
        return jnp.where(mask, jnp.concatenate([zb] * hpg, axis=0), jnp.zeros((), BF16))

    def cmm(a_cat, b):
        return jnp.dot(a_cat.astype(BF16), stack(b), preferred_element_type=F32)

    ti = lax.broadcasted_iota(jnp.int32, (L, L), 0)
    tj = lax.broadcasted_iota(jnp.int32, (L, L), 1)
    tri = (ti >= tj).astype(F32)
    ci_ = lax.broadcasted_iota(jnp.int32, (L, catw), 0)
    cj_ = lax.broadcasted_iota(jnp.int32, (L, catw), 1) % L
    strict = ci_ > cj_
    incl = ci_ >= cj_

    groups = range(ngroups)
    cols = [slice(grp * gw, (grp + 1) * gw) for grp in groups]

    xl = xl_ref[...]
    xl_prev = jnp.where(row_r == 0, carry["l"], pltpu.roll(xl, 1, 0))
    xl = xl + (xl_prev - xl) * mul_ref[...]
    xl_tanh = jnp.tanh(xl)
    xl_sig = _sigmoid(xl)
    wl_pre = [w0_ref[:, cs] + _dot_hp(xl_tanh, w2_ref[:, cs]) for cs in cols]
    a_pre = [a0_ref[:, cs] + _bdot(xl, a2_ref[:, cs]) for cs in cols]
    gate = [_bdot(xl_sig, g2_ref[:, cs]) for cs in cols]

    ctx = [dict() for _ in range(nchunk)]

    def setup_task(c):
        p = ctx[c]
        rs = slice(c * L, (c + 1) * L)
        rr = [shift_mix(xr_ref, "r", mur_ref, c, cs) for cs in cols]
        yield
        rk = [shift_mix(xk_ref, "k", muk_ref, c, cs) for cs in cols]
        yield
        rv = [shift_mix(xv_ref, "v", muv_ref, c, cs) for cs in cols]
        p["v"] = rv
        yield
        lw = [-jnp.exp(_log_sigmoid(z[rs]) - 0.5) for z in wl_pre]
        yield
        a = [_sigmoid(z[rs]) for z in a_pre]
        kk = [x * kkw_ref[:, cs] for x, cs in zip(rk, cols)]
        k2 = [x * (1.0 + (ai - 1.0) * kaw_ref[:, cs]) for x, ai, cs in zip(rk, a, cols)]
        yield
        sums = seg_sums([x * x for x in kk] + [r * k * rrk_ref[:, cs] for r, k, cs in zip(rr, k2, cols)])
        yield
        kk = [x / jnp.maximum(jnp.sqrt(s), 1e-12) for x, s in zip(kk, sums[:ngroups])]
        bv = [x * ai for x, ai in zip(kk, a)]
        p["bonus"] = [s * v for s, v in zip(sums[ngroups:], rv)]
        yield
        cum_c = [_dot_exact_lhs(tri, z) for z in lw]
        cum_l = [z[L - 1:L, :] for z in cum_c]
        p["w_l"] = [jnp.exp(z) for z in cum_l]
        yield
        e_neg = [jnp.exp(-z) for z in cum_c]
        yield
        bk_stack = [jnp.concatenate([stack(b * e), stack(k * e)], axis=0) for b, k, e in zip(bv, k2, e_neg)]
        yield
        at = [-x * jnp.exp(z - w) for x, z, w in zip(kk, cum_c, lw)]
        yield
        rt = [x * jnp.exp(z) for x, z in zip(rr, cum_c)]
        p["ar"] = [jnp.concatenate([x, y], axis=0) for x, y in zip(at, rt)]
        yield
        aa = [_bdot_nt(x, y) for x, y in zip(p["ar"], bk_stack)]
        yield
        e_end = [jnp.exp(zl - z) for zl, z in zip(cum_l, cum_c)]
        p["bk_end"] = [jnp.concatenate([b * e, k * e], axis=0) for b, k, e in zip(bv, k2, e_end)]
        yield
        p["nm"] = [jnp.where(strict, z[:L, :catw], 0.0) for z in aa]
        p["arb"] = [jnp.where(incl, z[L:, :catw], 0.0) for z in aa]
        yield
        ak = [jnp.concatenate([jnp.where(strict, z[:L, catw:], 0.0), jnp.where(incl, z[L:, catw:], 0.0)], axis=0)
              for z in aa]
        yield
        p["akv"] = [cmm(x, v) for x, v in zip(ak, p["v"])]

    def solve_task(c):
        yield from _tri_solve_cat(ctx[c]["nm"], cmm, L, ctx[c])

    def tail_task(c):
        p = ctx[c]
        rs = slice(c * L, (c + 1) * L)
        s0 = [s_ref[grp] for grp in groups]
        ps = [_bdot_nt(x, s) for x, s in zip(p["ar"], s0)]
        yield
        u = [cmm(t, z[:L] + w[:L]) for t, z, w in zip(p["t"], ps, p["akv"])]
        yield
        y = [z[L:] + cmm(x, ui) + w[L:] for z, x, ui, w in zip(ps, p["arb"], u, p["akv"])]
        yield
        upd = [_bdot_tn(jnp.concatenate([ui, v], axis=0), x) for ui, v, x in zip(u, p["v"], p["bk_end"])]
        yield
        for grp in groups:
            s_ref[grp] = s0[grp] * p["w_l"][grp] + jnp.where(same_head, upd[grp], 0.0)
        yield
        mu = [z * (1.0 / hd) for z in seg_sums(y)]
        yield
        d = [z - m for z, m in zip(y, mu)]
        var = [z * (1.0 / hd) for z in seg_sums([z * z for z in d])]
        yield
        for grp in groups:
            cs = cols[grp]
            yn = d[grp] * lax.rsqrt(var[grp] + R_GN_EPS) * gng_ref[:, cs] + gnb_ref[:, cs]
            out_ref[rs, cs] = ((yn + p["bonus"][grp]) * gate[grp][rs]).astype(out_ref.dtype)

    for k in range(nchunk + 2):
        tasks = []
        if 0 <= k - 2 < nchunk:
            tasks.append(tail_task(k - 2))
        if 0 <= k - 1 < nchunk:
            tasks.append(solve_task(k - 1))
        if k < nchunk:
            tasks.append(setup_task(k))
        _interleave(*tasks)


def _rwkv(pr, r_mu, w0, a0, kkw, kaw, rrk, gng, gnb, w2p, a2p, g2p, bsz, t):
    n = pr.shape[0]
    width = w0.shape[1]
    hd = R_HEAD_DIM
    gw = R_GROUP_HEADS * hd
    L = R_CHUNK
    R = L * R_CHUNKS_PER_STEP
    ns = t // R
    lr = w2p.shape[0]
    lblk = (3 * width) // lr
    rows = lambda b, c: b * ns + c
    xspec = lambda off: pl.BlockSpec((R, width), lambda b, c: (rows(b, c), off))
    pspec = lambda off: pl.BlockSpec((1, width), lambda b, c: (0, off))
    gspec = pl.BlockSpec((1, width), lambda b, c: (0, 0))
    wspec = pl.BlockSpec((lr, width), lambda b, c: (0, 0))
    return pl.pallas_call(
        functools.partial(_rwkv_kernel, L=L, nchunk=R_CHUNKS_PER_STEP, hd=hd, gw=gw),
        grid=(bsz, ns),
        in_specs=[
            xspec(0), xspec(1), xspec(2),
            pl.BlockSpec((R, lr), lambda b, c: (rows(b, c), lblk)),
            pspec(0), pspec(1), pspec(2),
            pl.BlockSpec((1, lr), lambda b, c: (0, lblk)),
            gspec, gspec, gspec, gspec, gspec, gspec, gspec,
            wspec, wspec, wspec,
        ],
        out_specs=pl.BlockSpec((R, width), lambda b, c: (rows(b, c), 0)),
        out_shape=jax.ShapeDtypeStruct((n, width), BF16),
        scratch_shapes=[
            pltpu.VMEM((width // gw, gw, gw), F32),
            pltpu.VMEM((8, width), F32),
            pltpu.VMEM((8, width), F32),
            pltpu.VMEM((8, width), F32),
            pltpu.VMEM((8, lr), F32),
        ],
        compiler_params=pltpu.CompilerParams(dimension_semantics=("arbitrary", "arbitrary"),
                                             vmem_limit_bytes=VMEM_LIMIT),
        name="rwkv",
    )(pr, pr, pr, pr, r_mu, r_mu, r_mu, r_mu, w0, a0, kkw, kaw, rrk, gng, gnb, w2p, a2p, g2p)


def _post_kernel(hm_ref, yr_ref, ga_ref, gb_ref, x_ref, wa_ref, wb_ref, wo_ref, g_ref, b_ref, o_ref, *, alpha):
    ya = jnp.dot(hm_ref[...], wa_ref[...], preferred_element_type=F32)
    yb = jnp.dot(yr_ref[...], wb_ref[...], preferred_element_type=F32)
    merged = _sigmoid(ga_ref[...].astype(F32)) * ya + _sigmoid(gb_ref[...].astype(F32)) * yb
    mix = jnp.dot(merged.astype(BF16), wo_ref[...], preferred_element_type=F32)
    o_ref[...] = _layer_norm(alpha * x_ref[...] + mix, g_ref[...], b_ref[...], LN_EPS)


def _post(hm, yr, pgate, x1, wa, wb, wo, g, b, alpha, tm=1024):
    n, d = x1.shape
    tile = lambda j: pl.BlockSpec((tm, d), lambda i: (i, j))
    wspec = pl.BlockSpec((d, d), lambda i: (0, 0), pipeline_mode=pl.Buffered(1))
    vspec = pl.BlockSpec((1, d), lambda i: (0, 0))
    return pl.pallas_call(
        functools.partial(_post_kernel, alpha=alpha),
        grid=(n // tm,),
        in_specs=[tile(0), tile(0), tile(0), tile(1), tile(0), wspec, wspec, wspec, vspec, vspec],
        out_specs=tile(0),
        out_shape=jax.ShapeDtypeStruct((n, d), F32),
        compiler_params=pltpu.CompilerParams(dimension_semantics=("arbitrary",), vmem_limit_bytes=VMEM_LIMIT),
        name="post",
    )(hm, yr, pgate, pgate, x1, wa, wb, wo, g, b)


def kernel(x, ffn1_w_gate, ffn1_w_up, ffn1_w_down, ln1_g, ln1_b, w_in, m_conv_w, m_conv_b, m_i_bias, m_f_bias,
           m_norm_g, r_mu, r_w0, r_w2, r_a0, r_a2, r_g2, r_k_k, r_k_a, r_r_k, r_gn_g, r_gn_b, w_branch_a,
           w_branch_b, w_out, ln2_g, ln2_b, ffn2_w_gate, ffn2_w_up, ffn2_w_down, ln3_g, ln3_b):
    bsz, t, d = x.shape
    depth = w_in.shape[0]
    alpha = (2 * depth) ** 0.25
    mw = d
    rw = d
    nh = M_HEADS
    row = lambda p: p.reshape(1, -1)

    cur = x.reshape(bsz * t, d)
    for l in range(depth):
        x1, x1b = _ffn_ln(cur, ffn1_w_gate[l].astype(BF16), ffn1_w_up[l].astype(BF16),
                          ffn1_w_down[l].astype(BF16), row(ln1_g[l]), row(ln1_b[l]), alpha)

        w = w_in[l].astype(BF16)
        o_gate = 4 * mw
        o_rc = o_gate + 2 * nh
        w_rest = w[:, o_rc:]
        w_gates = jnp.pad(w[:, o_gate:o_rc], ((0, 0), (0, 128 - 2 * nh)))
        gate_bias = jnp.pad(jnp.concatenate([m_i_bias[l], m_f_bias[l]]), (0, 128 - 2 * nh)).reshape(1, 128)
        pr, pgate, hm = _projm(x1b, w, w_rest, w_gates, m_conv_w[l], row(m_conv_b[l]), gate_bias,
                               row(m_norm_g[l]), 2 * d, bsz, t)

        zeros = lambda r: jnp.zeros((r, rw), F32)
        w2p = jnp.concatenate([r_w2[l], zeros(R_AAA_RANK + R_GATE_RANK)], axis=0)
        a2p = jnp.concatenate([zeros(R_DECAY_RANK), r_a2[l], zeros(R_GATE_RANK)], axis=0)
        g2p = jnp.concatenate([zeros(R_DECAY_RANK + R_AAA_RANK), r_g2[l]], axis=0)
        yr = _rwkv(pr, row(r_mu[l]), row(r_w0[l]), row(r_a0[l]), row(r_k_k[l]), row(r_k_a[l]), row(r_r_k[l]),
                   row(r_gn_g[l]), row(r_gn_b[l]), w2p, a2p, g2p, bsz, t)

        x2 = _post(hm, yr, pgate, x1, w_branch_a[l].astype(BF16), w_branch_b[l].astype(BF16),
                   w_out[l].astype(BF16), row(ln2_g[l]), row(ln2_b[l]), alpha)
        cur, _ = _ffn_ln(x2, ffn2_w_gate[l].astype(BF16), ffn2_w_up[l].astype(BF16),
                         ffn2_w_down[l].astype(BF16), row(ln3_g[l]), row(ln3_b[l]), alpha)
    return cur.reshape(bsz, t, d)
```

```python
import functools

import jax
import jax.numpy as jnp
from jax import lax
from jax.experimental import pallas as pl
from jax.experimental.pallas import tpu as pltpu

F32 = jnp.float32
BF16 = jnp.bfloat16

M_HEADS = 4
M_CONV = 4
M_CHUNK = 128
M_HALO = 16
R_HEAD_DIM = 64
R_DECAY_RANK = 64
R_AAA_RANK = 64
R_GATE_RANK = 128
R_GN_EPS = 64e-5
LN_EPS = 1e-5

R_CHUNK = 64
R_GROUP_HEADS = 2
R_CHUNKS_PER_STEP = 8
P_TILE = 256
P_COLS = 256
VMEM_LIMIT = 56 * 1024 * 1024


def _bdot(a, b):
    return jnp.dot(a.astype(BF16), b.astype(BF16), preferred_element_type=F32)


def _bdot_nt(a, b):
    return lax.dot_general(a.astype(BF16), b.astype(BF16), (((1,), (1,)), ((), ())),
                           preferred_element_type=F32)


def _bdot_tn(a, b):
    return lax.dot_general(a.astype(BF16), b.astype(BF16), (((0,), (0,)), ((), ())),
                           preferred_element_type=F32)


def _split3(x):
    hi = x.astype(BF16)
    r1 = x - hi.astype(F32)
    mid = r1.astype(BF16)
    lo = (r1 - mid.astype(F32)).astype(BF16)
    return hi, mid, lo


def _dot_exact_lhs(m01, x):
    m = m01.astype(BF16)
    hi, mid, lo = _split3(x)
    return (jnp.dot(m, hi, preferred_element_type=F32) + jnp.dot(m, mid, preferred_element_type=F32)
            + jnp.dot(m, lo, preferred_element_type=F32))


def _dot_hp(a, b):
    a_hi = a.astype(BF16)
    a_lo = (a - a_hi.astype(F32)).astype(BF16)
    b_hi = b.astype(BF16)
    b_lo = (b - b_hi.astype(F32)).astype(BF16)
    return (jnp.dot(a_hi, b_hi, preferred_element_type=F32) + jnp.dot(a_lo, b_hi, preferred_element_type=F32)
            + jnp.dot(a_hi, b_lo, preferred_element_type=F32))


def _sigmoid(x):
    return 0.5 * jnp.tanh(0.5 * x) + 0.5


def _log_sigmoid(x):
    return jnp.minimum(x, 0.0) - jnp.log1p(jnp.exp(-jnp.abs(x)))


def _layer_norm(y, g, b, eps):
    mu = jnp.mean(y, axis=-1, keepdims=True)
    d = y - mu
    var = jnp.mean(d * d, axis=-1, keepdims=True)
    return d * lax.rsqrt(var + eps) * g + b


def _ffn_ln_kernel(x_ref, wg_ref, wu_ref, wd_ref, g_ref, b_ref, o_ref, ob_ref, h_ref, *, alpha, tf):
    x = x_ref[...]
    xb = x.astype(BF16)
    d_ff = wg_ref.shape[1]
    for c in range(d_ff // tf):
        sl = slice(c * tf, (c + 1) * tf)
        gate = jnp.dot(xb, wg_ref[:, sl], preferred_element_type=F32)
        up = jnp.dot(xb, wu_ref[:, sl], preferred_element_type=F32)
        h_ref[:, sl] = (gate * _sigmoid(gate) * up).astype(BF16)
    y = alpha * x + 0.5 * jnp.dot(h_ref[...], wd_ref[...], preferred_element_type=F32)
    out = _layer_norm(y, g_ref[...], b_ref[...], LN_EPS)
    o_ref[...] = out
    ob_ref[...] = out.astype(BF16)


def _ffn_ln(x, wg, wu, wd, g, b, alpha, tm=1024, tf=256):
    n, d = x.shape
    d_ff = wg.shape[1]
    const = lambda i: (0, 0)
    return pl.pallas_call(
        functools.partial(_ffn_ln_kernel, alpha=alpha, tf=tf),
        grid=(n // tm,),
        in_specs=[
            pl.BlockSpec((tm, d), lambda i: (i, 0)),
            pl.BlockSpec((d, d_ff), const, pipeline_mode=pl.Buffered(1)),
            pl.BlockSpec((d, d_ff), const, pipeline_mode=pl.Buffered(1)),
            pl.BlockSpec((d_ff, d), const, pipeline_mode=pl.Buffered(1)),
            pl.BlockSpec((1, d), const),
            pl.BlockSpec((1, d), const),
        ],
        out_specs=[pl.BlockSpec((tm, d), lambda i: (i, 0)), pl.BlockSpec((tm, d), lambda i: (i, 0))],
        out_shape=[jax.ShapeDtypeStruct((n, d), F32), jax.ShapeDtypeStruct((n, d), BF16)],
        scratch_shapes=[pltpu.VMEM((tm, d_ff), BF16)],
        compiler_params=pltpu.CompilerParams(dimension_semantics=("arbitrary",), vmem_limit_bytes=VMEM_LIMIT),
        name="ffn_ln",
    )(x, wg, wu, wd, g, b)


def _projm_kernel(x_ref, wm_ref, wr_ref, wg_ref, cw_ref, cb_ref, gb_ref, ng_ref, of_ref, og_ref, hm_ref,
                  qkvo_s, gates_s, c_ref, n_ref, m_ref, pq_ref, pk_ref, *, nh, dh, L, pcols):
    tb = x_ref.shape[0]
    width = nh * dh
    n_m = qkvo_s.shape[2]
    n_f = of_ref.shape[1]
    n_g = og_ref.shape[1]
    halo = pq_ref.shape[0]
    step = pl.program_id(1)
    cur = lax.rem(step, jnp.int32(2))
    prv = 1 - cur

    @pl.when(step == 0)
    def _():
        pq_ref[...] = jnp.zeros_like(pq_ref)
        pk_ref[...] = jnp.zeros_like(pk_ref)
        c_ref[...] = jnp.zeros_like(c_ref)
        n_ref[...] = jnp.zeros_like(n_ref)
        m_ref[...] = jnp.zeros_like(m_ref)

    def proj_task():
        x = x_ref[...]
        for c0 in range(0, n_m, pcols):
            c1 = min(c0 + pcols, n_m)
            qkvo_s[cur, :, c0:c1] = jnp.dot(x, wm_ref[:, c0:c1], preferred_element_type=F32).astype(BF16)
            yield
        n_r = n_f - 128
        for c0 in range(0, n_r, pcols):
            c1 = min(c0 + pcols, n_r)
            of_ref[:, c0:c1] = jnp.dot(x, wr_ref[:, c0:c1], preferred_element_type=F32)
            yield
        gates = jnp.dot(x, wg_ref[...], preferred_element_type=F32)
        of_ref[:, n_r:] = gates
        gates_s[cur] = gates
        yield
        for c0 in range(0, n_g, pcols):
            c1 = min(c0 + pcols, n_g)
            og_ref[:, c0:c1] = jnp.dot(x, wr_ref[:, n_r + c0:n_r + c1],
                                       preferred_element_type=F32).astype(og_ref.dtype)
            yield

    sr = lax.broadcasted_iota(jnp.int32, ((M_CONV - 1) * L, halo + L), 0)
    sc = lax.broadcasted_iota(jnp.int32, ((M_CONV - 1) * L, halo + L), 1)
    shift = (sc == halo + sr % L - (sr // L + 1)).astype(BF16)
    half_w = 0.5 * cw_ref[...]
    half_b = 0.5 * cb_ref[...]
    ri = lax.broadcasted_iota(jnp.int32, (L, L), 0)
    ci = lax.broadcasted_iota(jnp.int32, (L, L), 1)
    causal = ri >= ci
    heads = range(nh)
    hs = [slice(h * dh, (h + 1) * dh) for h in heads]

    def mlstm_task(chunk):
        rows = slice(chunk * L, (chunk + 1) * L)

        def conv_silu(p_ref, w_off):
            x = qkvo_s[prv, rows, w_off:w_off + width]
            delayed = jnp.dot(shift, jnp.concatenate([p_ref[...], x], axis=0), preferred_element_type=F32)
            p_ref[...] = x[L - halo:, :]
            acc = half_b[:, w_off:w_off + width] + x.astype(F32) * half_w[M_CONV - 1:M_CONV, w_off:w_off + width]
            for j in range(1, M_CONV):
                acc = acc + delayed[(j - 1) * L:j * L] * half_w[M_CONV - 1 - j:M_CONV - j, w_off:w_off + width]
            return acc + acc * jnp.tanh(acc)

        q_all = conv_silu(pq_ref, 0) * (dh ** -0.5)
        yield
        k_all = conv_silu(pk_ref, width)
        yield
        gates = gates_s[prv, rows, :] + gb_ref[...]
        b_cols = _dot_exact_lhs(causal.astype(F32), _log_sigmoid(gates))
        gates_t = gates.T
        b_rows = b_cols.T
        yield

        q = [q_all[:, s] for s in hs]
        k = [k_all[:, s] for s in hs]
        v = [qkvo_s[prv, rows, 2 * width + h * dh:2 * width + (h + 1) * dh] for h in heads]
        i_col = [gates[:, h:h + 1] for h in heads]
        b_col = [b_cols[:, nh + h:nh + h + 1] for h in heads]
        i_row = [gates_t[h:h + 1, :] for h in heads]
        b_row = [b_rows[nh + h:nh + h + 1, :] for h in heads]
        g_tot = [b[L - 1:L, :] for b in b_col]
        c_prev = [c_ref[h] for h in heads]
        n_prev = [n_ref[8 * h:8 * h + 1, :] for h in heads]
        m_prev = [m_ref[8 * h:8 * h + 1, 0:1] for h in heads]

        a_col = [g_tot[h] - b_col[h] + i_col[h] for h in heads]
        m_loc = [jnp.max(z, axis=0, keepdims=True) for z in a_col]
        m_new = [jnp.maximum(g_tot[h] + m_prev[h], m_loc[h]) for h in heads]
        kw = [k[h] * jnp.exp(a_col[h] - m_new[h]) for h in heads]
        yield
        c_loc = [_bdot_tn(kw[h], v[h]) for h in heads]
        n_loc = [jnp.sum(z, axis=0, keepdims=True) for z in kw]
        yield

        d_log = [jnp.where(causal, b_col[h] - b_row[h] + i_row[h], -jnp.inf) for h in heads]
        inter = [b_col[h] + m_prev[h] for h in heads]
        m_t = [jnp.maximum(jnp.max(d_log[h], axis=-1, keepdims=True), inter[h]) for h in heads]
        yield
        s = [_bdot_nt(q[h], k[h]) * jnp.exp(d_log[h] - m_t[h]) for h in heads]
        yield
        s_inter = [jnp.exp(inter[h] - m_t[h]) for h in heads]
        num = [_bdot(s[h], v[h]) + s_inter[h] * _bdot(q[h], c_prev[h]) for h in heads]
        yield
        den = [jnp.sum(s[h], axis=-1, keepdims=True)
               + s_inter[h] * jnp.sum(q[h] * n_prev[h], axis=-1, keepdims=True) for h in heads]
        yield
        hid = [num[h] / jnp.maximum(jnp.abs(den[h]), jnp.exp(-m_t[h])) for h in heads]
        yield

        for h in heads:
            s_old = jnp.exp(g_tot[h] + m_prev[h] - m_new[h])
            c_ref[h] = s_old * c_prev[h] + c_loc[h]
            n_ref[8 * h:8 * h + 1, :] = s_old * n_prev[h] + n_loc[h]
            m_ref[8 * h:8 * h + 1, :] = jnp.broadcast_to(m_new[h], (1, m_ref.shape[1]))
        yield

        for h in heads:
            mu = jnp.mean(hid[h], axis=-1, keepdims=True)
            d = hid[h] - mu
            var = jnp.mean(d * d, axis=-1, keepdims=True)
            hn = d * lax.rsqrt(var + LN_EPS)
            gate_o = _sigmoid(qkvo_s[prv, rows, 3 * width + h * dh:3 * width + (h + 1) * dh].astype(F32))
            hm_ref[rows, hs[h]] = (hn * ng_ref[:, hs[h]] * gate_o).astype(hm_ref.dtype)
            yield

    def mlstm_all():
        for chunk in range(tb // L):
            yield from mlstm_task(chunk)

    last = pl.num_programs(1) - 1

    @pl.when(step == 0)
    def _():
        _interleave(proj_task())

    @pl.when(jnp.logical_and(step > 0, step < last))
    def _():
        _interleave(mlstm_all(), proj_task())

    @pl.when(step == last)
    def _():
        _interleave(mlstm_all())


def _projm(xb, w_full, w_rest, w_gates, conv_w, conv_b, gate_bias, norm_g, n_g, bsz, t):
    n, d = xb.shape
    width = norm_g.shape[1]
    nh = M_HEADS
    n_m = 4 * width
    n_f = w_rest.shape[1] - n_g + 128
    tb = P_TILE
    nt = t // tb
    tile_in = lambda b, j: (b * nt + jnp.minimum(j, nt - 1), 0)
    tile_out = lambda b, j: (b * nt + jnp.maximum(j - 1, 0), 0)
    const = lambda b, j: (0, 0)
    return pl.pallas_call(
        functools.partial(_projm_kernel, nh=nh, dh=width // nh, L=M_CHUNK, pcols=P_COLS),
        grid=(bsz, nt + 1),
        in_specs=[
            pl.BlockSpec((tb, d), tile_in),
            pl.BlockSpec((d, n_m), const, pipeline_mode=pl.Buffered(1)),
            pl.BlockSpec(w_rest.shape, const, pipeline_mode=pl.Buffered(1)),
            pl.BlockSpec(w_gates.shape, const, pipeline_mode=pl.Buffered(1)),
            pl.BlockSpec((M_CONV, 2 * width), const),
            pl.BlockSpec((1, 2 * width), const),
            pl.BlockSpec((1, 128), const),
            pl.BlockSpec((1, width), const),
        ],
        out_specs=[pl.BlockSpec((tb, n_f), tile_in), pl.BlockSpec((tb, n_g), tile_in),
                   pl.BlockSpec((tb, width), tile_out)],
        out_shape=[jax.ShapeDtypeStruct((n, n_f), F32), jax.ShapeDtypeStruct((n, n_g), BF16),
                   jax.ShapeDtypeStruct((n, width), BF16)],
        scratch_shapes=[
            pltpu.VMEM((2, tb, n_m), BF16),
            pltpu.VMEM((2, tb, 128), F32),
            pltpu.VMEM((nh, width // nh, width // nh), F32),
            pltpu.VMEM((8 * nh, width // nh), F32),
            pltpu.VMEM((8 * nh, 128), F32),
            pltpu.VMEM((M_HALO, width), BF16),
            pltpu.VMEM((M_HALO, width), BF16),
        ],
        compiler_params=pltpu.CompilerParams(dimension_semantics=("arbitrary", "arbitrary"),
                                             vmem_limit_bytes=VMEM_LIMIT),
        name="proj_mlstm",
    )(xb, w_full, w_rest, w_gates, conv_w, conv_b, gate_bias, norm_g)


def _tri_solve_cat(nms, cmm, L, out):
    shape = nms[0].shape
    ri = lax.broadcasted_iota(jnp.int32, shape, 0)
    ci = lax.broadcasted_iota(jnp.int32, shape, 1) % L
    blk = lambda w: (ri // w) == (ci // w)
    eye = (ri == ci).astype(F32)
    diag = blk(16)
    nds = [jnp.where(diag, nm, 0.0) for nm in nms]
    ts = [eye + nd for nd in nds]
    pws = [cmm(nd, nd) for nd in nds]
    yield
    for _ in range(2):
        both = [cmm(jnp.concatenate([p, t], axis=0), p) for p, t in zip(pws, ts)]
        pws = [b[:L] for b in both]
        ts = [t + b[L:] for t, b in zip(ts, both)]
        yield
    ts = [t + cmm(t, p) for t, p in zip(ts, pws)]
    yield
    ys = [cmm(jnp.where(diag, 0.0, nm), t) for nm, t in zip(nms, ts)]
    yield
    power = 1
    while 2 * power < L // 16:
        both = [cmm(jnp.concatenate([y, t], axis=0), y) for y, t in zip(ys, ts)]
        ys = [b[:L] for b in both]
        ts = [t + b[L:] for t, b in zip(ts, both)]
        power *= 2
        yield
    out["t"] = [t + cmm(t, y) for t, y in zip(ts, ys)]


def _interleave(*tasks):
    tasks = list(tasks)
    while tasks:
        for task in list(tasks):
            try:
                next(task)
            except StopIteration:
                tasks.remove(task)


def _rwkv_kernel(xr_ref, xk_ref, xv_ref, xl_ref, mur_ref, muk_ref, muv_ref, mul_ref,
                 w0_ref, a0_ref, kkw_ref, kaw_ref, rrk_ref, gng_ref, gnb_ref,
                 w2_ref, a2_ref, g2_ref, out_ref,
                 s_ref, pr_ref, pk_ref, pv_ref, pl_ref, *, L, nchunk, hd, gw):
    R = xr_ref.shape[0]
    width = xr_ref.shape[1]
    ngroups = width // gw
    hpg = gw // hd
    step = pl.program_id(1)

    @pl.when(step == 0)
    def _():
        s_ref[...] = jnp.zeros_like(s_ref)
        pr_ref[...] = jnp.zeros_like(pr_ref)
        pk_ref[...] = jnp.zeros_like(pk_ref)
        pv_ref[...] = jnp.zeros_like(pv_ref)
        pl_ref[...] = jnp.zeros_like(pl_ref)

    row_r = lax.broadcasted_iota(jnp.int32, (R, 1), 0)
    row_l = lax.broadcasted_iota(jnp.int32, (L, 1), 0)
    carry = {}
    for key, x_ref, p_ref in (("r", xr_ref, pr_ref), ("k", xk_ref, pk_ref), ("v", xv_ref, pv_ref),
                              ("l", xl_ref, pl_ref)):
        carry[key] = p_ref[0:1, :]
        p_ref[0:1, :] = x_ref[R - 1:R, :]

    def shift_mix(x_ref, key, mu_ref, c, cs):
        x = x_ref[c * L:(c + 1) * L, cs]
        first = carry[key][:, cs] if c == 0 else x_ref[c * L - 1:c * L, cs]
        prev = jnp.where(row_l == 0, first, pltpu.roll(x, 1, 0))
        return x + (prev - x) * mu_ref[:, cs]

    catw = hpg * L
    li = lax.broadcasted_iota(jnp.int32, (gw, gw), 0)
    lj = lax.broadcasted_iota(jnp.int32, (gw, gw), 1)
    same_head = (li // hd) == (lj // hd)
    si = lax.broadcasted_iota(jnp.int32, (catw, 1), 0) // L
    stack_full = si == lax.broadcasted_iota(jnp.int32, (1, gw), 1) // hd
    stack_cat = si == lax.broadcasted_iota(jnp.int32, (1, catw), 1) // L
    lane_head = lax.broadcasted_iota(jnp.int32, (1, gw), 1) // hd

    def seg_sums(zs):
        outs = []
        for z in zs:
            sums = [jnp.sum(jnp.where(lane_head == h, z, 0.0), axis=-1, keepdims=True) for h in range(hpg)]
            acc = jnp.broadcast_to(sums[hpg - 1], z.shape)
            for h in range(hpg - 1):
                acc = jnp.where(lane_head == h, sums[h], acc)
            outs.append(acc)
        return outs

    def stack(z):
        zb = z.astype(BF16)
        mask = stack_full if z.shape[1] == gw else stack_cat
        return jnp.where(mask, jnp.concatenate([zb] * hpg, axis=0), jnp.zeros((), BF16))

    def cmm(a_cat, b):
        return jnp.dot(a_cat.astype(BF16), stack(b), preferred_element_type=F32)

    ti = lax.broadcasted_iota(jnp.int32, (L, L), 0)
    tj = lax.broadcasted_iota(jnp.int32, (L, L), 1)
    tri = (ti >= tj).astype(F32)
    ci_ = lax.broadcasted_iota(jnp.int32, (L, catw), 0)
    cj_ = lax.broadcasted_iota(jnp.int32, (L, catw), 1) % L
    strict = ci_ > cj_
    incl = ci_ >= cj_

    groups = range(ngroups)
    cols = [slice(grp * gw, (grp + 1) * gw) for grp in groups]

    xl = xl_ref[...]
    xl_prev = jnp.where(row_r == 0, carry["l"], pltpu.roll(xl, 1, 0))
    xl = xl + (xl_prev - xl) * mul_ref[...]
    xl_tanh = jnp.tanh(xl)
    xl_sig = _sigmoid(xl)
    wl_pre = [w0_ref[:, cs] + _dot_hp(xl_tanh, w2_ref[:, cs]) for cs in cols]
    a_pre = [a0_ref[:, cs] + _bdot(xl, a2_ref[:, cs]) for cs in cols]
    gate = [_bdot(xl_sig, g2_ref[:, cs]) for cs in cols]

    ctx = [dict() for _ in range(nchunk)]

    def setup_task(c):
        p = ctx[c]
        rs = slice(c * L, (c + 1) * L)
        rr = [shift_mix(xr_ref, "r", mur_ref, c, cs) for cs in cols]
        yield
        rk = [shift_mix(xk_ref, "k", muk_ref, c, cs) for cs in cols]
        yield
        rv = [shift_mix(xv_ref, "v", muv_ref, c, cs) for cs in cols]
        p["v"] = rv
        yield
        lw = [-jnp.exp(_log_sigmoid(z[rs]) - 0.5) for z in wl_pre]
        yield
        a = [_sigmoid(z[rs]) for z in a_pre]
        kk = [x * kkw_ref[:, cs] for x, cs in zip(rk, cols)]
        k2 = [x * (1.0 + (ai - 1.0) * kaw_ref[:, cs]) for x, ai, cs in zip(rk, a, cols)]
        yield
        sums = seg_sums([x * x for x in kk] + [r * k * rrk_ref[:, cs] for r, k, cs in zip(rr, k2, cols)])
        yield
        kk = [x * lax.rsqrt(jnp.maximum(s, 1e-24)) for x, s in zip(kk, sums[:ngroups])]
        bv = [x * ai for x, ai in zip(kk, a)]
        p["bonus"] = [s * v for s, v in zip(sums[ngroups:], rv)]
        yield
        cum_c = [_dot_exact_lhs(tri, z) for z in lw]
        cum_l = [z[L - 1:L, :] for z in cum_c]
        p["w_l"] = [jnp.exp(z) for z in cum_l]
        yield
        e_neg = [jnp.exp(-z) for z in cum_c]
        yield
        bk_stack = [jnp.concatenate([stack(b * e), stack(k * e)], axis=0) for b, k, e in zip(bv, k2, e_neg)]
        yield
        at = [-x * jnp.exp(z - w) for x, z, w in zip(kk, cum_c, lw)]
        yield
        rt = [x * jnp.exp(z) for x, z in zip(rr, cum_c)]
        p["ar"] = [jnp.concatenate([x, y], axis=0) for x, y in zip(at, rt)]
        yield
        aa = [_bdot_nt(x, y) for x, y in zip(p["ar"], bk_stack)]
        yield
        e_end = [jnp.exp(zl - z) for zl, z in zip(cum_l, cum_c)]
        p["bk_end"] = [jnp.concatenate([b * e, k * e], axis=0) for b, k, e in zip(bv, k2, e_end)]
        yield
        p["nm"] = [jnp.where(strict, z[:L, :catw], 0.0) for z in aa]
        p["arb"] = [jnp.where(incl, z[L:, :catw], 0.0) for z in aa]
        yield
        ak = [jnp.concatenate([jnp.where(strict, z[:L, catw:], 0.0), jnp.where(incl, z[L:, catw:], 0.0)], axis=0)
              for z in aa]
        yield
        p["akv"] = [cmm(x, v) for x, v in zip(ak, p["v"])]

    def solve_task(c):
        yield from _tri_solve_cat(ctx[c]["nm"], cmm, L, ctx[c])

    def tail_task(c):
        p = ctx[c]
        rs = slice(c * L, (c + 1) * L)
        s0 = [s_ref[grp] for grp in groups]
        ps = [_bdot_nt(x, s) for x, s in zip(p["ar"], s0)]
        yield
        u = [cmm(t, z[:L] + w[:L]) for t, z, w in zip(p["t"], ps, p["akv"])]
        yield
        y = [z[L:] + cmm(x, ui) + w[L:] for z, x, ui, w in zip(ps, p["arb"], u, p["akv"])]
        yield
        upd = [_bdot_tn(jnp.concatenate([ui, v], axis=0), x) for ui, v, x in zip(u, p["v"], p["bk_end"])]
        yield
        for grp in groups:
            s_ref[grp] = s0[grp] * p["w_l"][grp] + jnp.where(same_head, upd[grp], 0.0)
        yield
        mu = [z * (1.0 / hd) for z in seg_sums(y)]
        yield
        d = [z - m for z, m in zip(y, mu)]
        var = [z * (1.0 / hd) for z in seg_sums([z * z for z in d])]
        yield
        for grp in groups:
            cs = cols[grp]
            yn = d[grp] * lax.rsqrt(var[grp] + R_GN_EPS) * gng_ref[:, cs] + gnb_ref[:, cs]
            out_ref[rs, cs] = ((yn + p["bonus"][grp]) * gate[grp][rs]).astype(out_ref.dtype)

    for k in range(nchunk + 2):
        tasks = []
        if 0 <= k - 2 < nchunk:
            tasks.append(tail_task(k - 2))
        if 0 <= k - 1 < nchunk:
            tasks.append(solve_task(k - 1))
        if k < nchunk:
            tasks.append(setup_task(k))
        _interleave(*tasks)


def _rwkv(pr, r_mu, w0, a0, kkw, kaw, rrk, gng, gnb, w2p, a2p, g2p, bsz, t):
    n = pr.shape[0]
    width = w0.shape[1]
    hd = R_HEAD_DIM
    gw = R_GROUP_HEADS * hd
    L = R_CHUNK
    R = L * R_CHUNKS_PER_STEP
    ns = t // R
    lr = w2p.shape[0]
    lblk = (3 * width) // lr
    rows = lambda b, c: b * ns + c
    xspec = lambda off: pl.BlockSpec((R, width), lambda b, c: (rows(b, c), off))
    pspec = lambda off: pl.BlockSpec((1, width), lambda b, c: (0, off))
    gspec = pl.BlockSpec((1, width), lambda b, c: (0, 0))
    wspec = pl.BlockSpec((lr, width), lambda b, c: (0, 0))
    return pl.pallas_call(
        functools.partial(_rwkv_kernel, L=L, nchunk=R_CHUNKS_PER_STEP, hd=hd, gw=gw),
        grid=(bsz, ns),
        in_specs=[
            xspec(0), xspec(1), xspec(2),
            pl.BlockSpec((R, lr), lambda b, c: (rows(b, c), lblk)),
            pspec(0), pspec(1), pspec(2),
            pl.BlockSpec((1, lr), lambda b, c: (0, lblk)),
            gspec, gspec, gspec, gspec, gspec, gspec, gspec,
            wspec, wspec, wspec,
        ],
        out_specs=pl.BlockSpec((R, width), lambda b, c: (rows(b, c), 0)),
        out_shape=jax.ShapeDtypeStruct((n, width), BF16),
        scratch_shapes=[
            pltpu.VMEM((width // gw, gw, gw), F32),
            pltpu.VMEM((8, width), F32),
            pltpu.VMEM((8, width), F32),
            pltpu.VMEM((8, width), F32),
            pltpu.VMEM((8, lr), F32),
        ],
        compiler_params=pltpu.CompilerParams(dimension_semantics=("arbitrary", "arbitrary"),
                                             vmem_limit_bytes=VMEM_LIMIT),
        name="rwkv",
    )(pr, pr, pr, pr, r_mu, r_mu, r_mu, r_mu, w0, a0, kkw, kaw, rrk, gng, gnb, w2p, a2p, g2p)


def _post_kernel(hm_ref, yr_ref, ga_ref, gb_ref, x_ref, wa_ref, wb_ref, wo_ref, g_ref, b_ref, o_ref, *, alpha):
    ya = jnp.dot(hm_ref[...], wa_ref[...], preferred_element_type=F32)
    yb = jnp.dot(yr_ref[...], wb_ref[...], preferred_element_type=F32)
    merged = _sigmoid(ga_ref[...].astype(F32)) * ya + _sigmoid(gb_ref[...].astype(F32)) * yb
    mix = jnp.dot(merged.astype(BF16), wo_ref[...], preferred_element_type=F32)
    o_ref[...] = _layer_norm(alpha * x_ref[...] + mix, g_ref[...], b_ref[...], LN_EPS)


def _post(hm, yr, pgate, x1, wa, wb, wo, g, b, alpha, tm=1024):
    n, d = x1.shape
    tile = lambda j: pl.BlockSpec((tm, d), lambda i: (i, j))
    wspec = pl.BlockSpec((d, d), lambda i: (0, 0), pipeline_mode=pl.Buffered(1))
    vspec = pl.BlockSpec((1, d), lambda i: (0, 0))
    return pl.pallas_call(
        functools.partial(_post_kernel, alpha=alpha),
        grid=(n // tm,),
        in_specs=[tile(0), tile(0), tile(0), tile(1), tile(0), wspec, wspec, wspec, vspec, vspec],
        out_specs=tile(0),
        out_shape=jax.ShapeDtypeStruct((n, d), F32),
        compiler_params=pltpu.CompilerParams(dimension_semantics=("arbitrary",), vmem_limit_bytes=VMEM_LIMIT),
        name="post",
    )(hm, yr, pgate, pgate, x1, wa, wb, wo, g, b)


def kernel(x, ffn1_w_gate, ffn1_w_up, ffn1_w_down, ln1_g, ln1_b, w_in, m_conv_w, m_conv_b, m_i_bias, m_f_bias,
           m_norm_g, r_mu, r_w0, r_w2, r_a0, r_a2, r_g2, r_k_k, r_k_a, r_r_k, r_gn_g, r_gn_b, w_branch_a,
           w_branch_b, w_out, ln2_g, ln2_b, ffn2_w_gate, ffn2_w_up, ffn2_w_down, ln3_g, ln3_b):
    bsz, t, d = x.shape
    depth = w_in.shape[0]
    alpha = (2 * depth) ** 0.25
    mw = d
    rw = d
    nh = M_HEADS
    row = lambda p: p.reshape(1, -1)

    cur = x.reshape(bsz * t, d)
    for l in range(depth):
        x1, x1b = _ffn_ln(cur, ffn1_w_gate[l].astype(BF16), ffn1_w_up[l].astype(BF16),
                          ffn1_w_down[l].astype(BF16), row(ln1_g[l]), row(ln1_b[l]), alpha)

        w = w_in[l].astype(BF16)
        o_gate = 4 * mw
        o_rc = o_gate + 2 * nh
        w_rest = w[:, o_rc:]
        w_gates = jnp.pad(w[:, o_gate:o_rc], ((0, 0), (0, 128 - 2 * nh)))
        gate_bias = jnp.pad(jnp.concatenate([m_i_bias[l], m_f_bias[l]]), (0, 128 - 2 * nh)).reshape(1, 128)
        pr, pgate, hm = _projm(x1b, w, w_rest, w_gates, m_conv_w[l], row(m_conv_b[l]), gate_bias,
                               row(m_norm_g[l]), 2 * d, bsz, t)

        zeros = lambda r: jnp.zeros((r, rw), F32)
        w2p = jnp.concatenate([r_w2[l], zeros(R_AAA_RANK + R_GATE_RANK)], axis=0)
        a2p = jnp.concatenate([zeros(R_DECAY_RANK), r_a2[l], zeros(R_GATE_RANK)], axis=0)
        g2p = jnp.concatenate([zeros(R_DECAY_RANK + R_AAA_RANK), r_g2[l]], axis=0)
        yr = _rwkv(pr, row(r_mu[l]), row(r_w0[l]), row(r_a0[l]), row(r_k_k[l]), row(r_k_a[l]), row(r_r_k[l]),
                   row(r_gn_g[l]), row(r_gn_b[l]), w2p, a2p, g2p, bsz, t)

        x2 = _post(hm, yr, pgate, x1, w_branch_a[l].astype(BF16), w_branch_b[l].astype(BF16),
                   w_out[l].astype(BF16), row(ln2_g[l]), row(ln2_b[l]), alpha)
        cur, _ = _ffn_ln(x2, ffn2_w_gate[l].astype(BF16), ffn2_w_up[l].astype(BF16),
                         ffn2_w_down[l].astype(BF16), row(ln3_g[l]), row(ln3_b[l]), alpha)
    return cur.reshape(bsz, t, d)
```
